```python
import math
import jax, jax.numpy as jnp
from jax import lax
import numpy as np

D_MODEL = 1024
BATCH = 4
SEQ = 8192
DEPTH = 2

HEAD_DIM = 64
D_MIX = D_MODEL
N_MIXERS = 4
N_GROUP_HEADS = D_MIX // (N_MIXERS * HEAD_DIM)
QB = 128
NEG = -1e30
TINY = 1e-30
BIG = 1e9
RMS_EPS = 1e-6

SWA_HEADS = N_GROUP_HEADS
SWA_KV_HEADS = 2
SWA_WINDOW = 128
MLA_HEADS = N_GROUP_HEADS
MLA_Q_RANK = 3 * D_MODEL // 16
MLA_KV_RANK = D_MODEL // 8
MLA_NOPE = 64
MLA_ROPE = 32
MLA_V = 64
ROPE_THETA = 10000.0
NSA_HEADS = N_GROUP_HEADS
NSA_CMP_LEN = 32
NSA_CMP_STRIDE = 16
NSA_CMP_HIDDEN = 128
NSA_SEL_BLOCK = 64
NSA_TOPN = 16
NSA_WINDOW = 512
NSA_N_BRANCH = 3
MOBA_HEADS = N_GROUP_HEADS
MOBA_BLOCK = 256
MOBA_TOPK = 3
D_FF = 4 * D_MODEL
N_ALIBI = SWA_HEADS + NSA_HEADS + MOBA_HEADS
N_ADA = 6

SWA_COLS = (SWA_HEADS + 2 * SWA_KV_HEADS) * HEAD_DIM
MLA_COLS = MLA_Q_RANK + MLA_KV_RANK + MLA_ROPE
NSA_COLS = NSA_HEADS * HEAD_DIM + 2 * NSA_N_BRANCH * HEAD_DIM + NSA_N_BRANCH * NSA_HEADS
MOBA_COLS = 3 * MOBA_HEADS * HEAD_DIM
D_IN = SWA_COLS + MLA_COLS + NSA_COLS + MOBA_COLS

kernel_name = 'hybrid_parallel_heads_swa_mla_nsa_moba'


def alibi_slopes():
    s = np.array([2.0 ** (-8.0 * (i + 1) / N_ALIBI) for i in range(N_ALIBI)], np.float32)
    return jnp.asarray(s[0::3]), jnp.asarray(s[1::3]), jnp.asarray(s[2::3])


def split_last(a, sizes):
    offs = np.cumsum([0] + list(sizes))
    return [a[..., int(offs[i]):int(offs[i + 1])] for i in range(len(sizes))]


def heads(t, n):
    B, S, W = t.shape
    return t.reshape(B, S, n, W // n).transpose(0, 2, 1, 3)


def merge_heads(t):
    B, H, S, d = t.shape
    return t.transpose(0, 2, 1, 3).reshape(B, S, H * d)


def unblock(o):
    nq, B, H, qb, d = o.shape
    return o.transpose(1, 2, 0, 3, 4).reshape(B, H, nq * qb, d)


def rmsnorm(x, g):
    xf = x.astype(jnp.float32)
    y = xf * lax.rsqrt(jnp.mean(xf * xf, axis=-1, keepdims=True) + RMS_EPS)
    return (y * g.astype(jnp.float32)).astype(x.dtype)


def rope(x, pos):
    half = x.shape[-1] // 2
    freqs = ROPE_THETA ** (-jnp.arange(half, dtype=jnp.float32) / half)
    ang = pos[:, None] * freqs[None, :]
    cos, sin = jnp.cos(ang), jnp.sin(ang)
    x1 = x[..., :half].astype(jnp.float32)
    x2 = x[..., half:].astype(jnp.float32)
    return jnp.concatenate([x1 * cos - x2 * sin, x1 * sin + x2 * cos], axis=-1).astype(x.dtype)


def masked_softmax(s, mask, sink=None):
    s = jnp.where(mask, s.astype(jnp.float32), NEG)
    m = jnp.max(s, axis=-1, keepdims=True)
    if sink is not None:
        m = jnp.maximum(m, sink)
    e = jnp.where(mask, jnp.exp(s - m), 0.0)
    den = jnp.sum(e, axis=-1, keepdims=True)
    if sink is not None:
        den = den + jnp.exp(sink - m)
    return e / jnp.maximum(den, TINY)


def banded_attention(q, k, v, window, slopes, sink=None):
    B, KVH, G, S, d = q.shape
    nq = S // QB
    nb = window // QB
    kw = (nb + 1) * QB

    def band(t):
        tb = t.reshape(B, KVH, nq, QB, d)
        tp = jnp.pad(tb, ((0, 0), (0, 0), (nb, 0), (0, 0), (0, 0)))
        return jnp.concatenate([tp[:, :, i:i + nq] for i in range(nb + 1)], axis=3)

    kb, vb = band(k), band(v)
    qb = q.reshape(B, KVH, G, nq, QB, d)
    s = jnp.einsum('bkgnqd,bknsd->bkgnqs', qb, kb) * (d ** -0.5)
    dist = nb * QB + jnp.arange(QB)[:, None] - jnp.arange(kw)[None, :]
    key_pos = (jnp.arange(nq)[:, None] - nb) * QB + jnp.arange(kw)[None, :]
    mask = ((dist >= 0) & (dist < window))[None, :, :] & (key_pos >= 0)[:, None, :]
    bias = -slopes.astype(jnp.float32)[:, :, None, None, None] * dist
    sink_b = None if sink is None else sink.astype(jnp.float32)[None, :, :, None, None, None]
    p = masked_softmax(s.astype(jnp.float32) + bias, mask, sink_b).astype(v.dtype)
    o = jnp.einsum('bkgnqs,bknsd->bkgnqd', p, vb)
    return o.reshape(B, KVH * G, S, d)


def causal_attention_blocks(q, k, v, scale):
    B, H, S, dq = q.shape
    nq = S // QB
    qb = q.reshape(B, H, nq, QB, dq).transpose(2, 0, 1, 3, 4)
    kpos = jnp.arange(S)

    def one(args):
        qi, n = args
        tq = n * QB + jnp.arange(QB)
        s = jnp.einsum('bhqd,bhsd->bhqs', qi, k) * scale
        p = masked_softmax(s, kpos[None, :] <= tq[:, None]).astype(v.dtype)
        return jnp.einsum('bhqs,bhsd->bhqd', p, v)

    return unblock(lax.map(one, (qb, jnp.arange(nq))))


def swa_mixer(z, sinks, slopes):
    G = SWA_HEADS // SWA_KV_HEADS
    q, k, v = split_last(z, [SWA_HEADS * HEAD_DIM, SWA_KV_HEADS * HEAD_DIM, SWA_KV_HEADS * HEAD_DIM])
    q = heads(q, SWA_HEADS)
    B, _, S, d = q.shape
    q = q.reshape(B, SWA_KV_HEADS, G, S, d)
    o = banded_attention(q, heads(k, SWA_KV_HEADS), heads(v, SWA_KV_HEADS), SWA_WINDOW,
                         slopes.reshape(SWA_KV_HEADS, G), sinks.reshape(SWA_KV_HEADS, G))
    return merge_heads(o)


def mla_mixer(z, q_norm_g, kv_norm_g, w_uq, w_ukv):
    cq, ckv, k_pe = split_last(z, [MLA_Q_RANK, MLA_KV_RANK, MLA_ROPE])
    B, S, _ = z.shape
    pos = jnp.arange(S, dtype=jnp.float32)
    q = heads(rmsnorm(cq, q_norm_g) @ w_uq, MLA_HEADS)
    kv = heads(rmsnorm(ckv, kv_norm_g) @ w_ukv, MLA_HEADS)
    q = jnp.concatenate([q[..., :MLA_NOPE], rope(q[..., MLA_NOPE:], pos)], axis=-1)
    k_pe = jnp.broadcast_to(rope(k_pe, pos)[:, None], (B, MLA_HEADS, S, MLA_ROPE))
    k = jnp.concatenate([kv[..., :MLA_NOPE], k_pe], axis=-1)
    v = kv[..., MLA_NOPE:]
    o = causal_attention_blocks(q, k, v, (MLA_NOPE + MLA_ROPE) ** -0.5)
    return merge_heads(o)


def nsa_mixer(z, slopes, pos_k, pos_v, ck_w1, ck_w2, cv_w1, cv_w2):
    d, H = HEAD_DIM, NSA_HEADS
    q, kc, vc, ks, vs, kw, vw, gl = split_last(z, [H * d, d, d, d, d, d, d, NSA_N_BRANCH * H])
    B, S, _ = z.shape
    q = heads(q, H)
    scale = d ** -0.5
    ratio = NSA_CMP_LEN // NSA_CMP_STRIDE
    n_cmp = S // NSA_CMP_STRIDE - ratio + 1

    def compress(t, pos_emb, w1, w2):
        ch = t.reshape(B, S // NSA_CMP_STRIDE, NSA_CMP_STRIDE, d)
        blk = jnp.concatenate([ch[:, i:i + n_cmp] for i in range(ratio)], axis=2) + pos_emb
        return jax.nn.gelu(blk.reshape(B, n_cmp, NSA_CMP_LEN * d) @ w1) @ w2

    k_cmp = compress(kc, pos_k, ck_w1, ck_w2)
    v_cmp = compress(vc, pos_v, cv_w1, cv_w2)
    cmp_end = jnp.arange(n_cmp) * NSA_CMP_STRIDE + NSA_CMP_LEN - 1
    n_sel = S // NSA_SEL_BLOCK
    topn = min(NSA_TOPN, n_sel)
    sel_ratio = NSA_SEL_BLOCK // NSA_CMP_STRIDE
    n_off = sel_ratio + ratio - 1
    k_sel = ks.reshape(B, n_sel, NSA_SEL_BLOCK, d)
    v_sel = vs.reshape(B, n_sel, NSA_SEL_BLOCK, d)
    nq = S // QB
    qb = q.reshape(B, H, nq, QB, d).transpose(2, 0, 1, 3, 4)
    m = slopes.astype(jnp.float32)[None, :, None, None]
    blk_ids = jnp.arange(n_sel)
    bi = jnp.arange(B)[:, None, None]
    nk = topn * NSA_SEL_BLOCK

    def one(args):
        qi, n = args
        tq = n * QB + jnp.arange(QB)
        dist_c = tq[:, None] - cmp_end[None, :]
        s_c = jnp.einsum('bhqd,bcd->bhqc', qi, k_cmp) * scale
        p_c = masked_softmax(s_c.astype(jnp.float32) - m * dist_c, dist_c >= 0)
        o_c = jnp.einsum('bhqc,bcd->bhqd', p_c.astype(v_cmp.dtype), v_cmp)
        pp = jnp.pad(p_c.sum(axis=1), ((0, 0), (0, 0), (ratio - 1, sel_ratio)))
        imp = sum(pp[..., o::sel_ratio][..., :n_sel] for o in range(n_off))
        cur = tq // NSA_SEL_BLOCK
        cand = blk_ids[None, :] <= cur[:, None]
        forced = (blk_ids[None, :] == 0) | (blk_ids[None, :] == cur[:, None]) | (blk_ids[None, :] == cur[:, None] - 1)
        imp = jnp.where(cand, jnp.where(forced, BIG, imp), NEG)
        _, idx = lax.top_k(imp, topn)
        valid = jnp.take_along_axis(jnp.broadcast_to(cand, imp.shape), idx, axis=-1)
        kg = k_sel[bi, idx].reshape(B, QB, nk, d)
        vg = v_sel[bi, idx].reshape(B, QB, nk, d)
        key_pos = (idx[..., None] * NSA_SEL_BLOCK + jnp.arange(NSA_SEL_BLOCK)).reshape(B, QB, nk)
        dist_s = tq[None, :, None] - key_pos
        mask_s = (jnp.repeat(valid, NSA_SEL_BLOCK, axis=-1) & (dist_s >= 0))[:, None]
        s_s = jnp.einsum('bhqd,bqkd->bhqk', qi, kg) * scale
        p_s = masked_softmax(s_s.astype(jnp.float32) - m * dist_s[:, None], mask_s)
        o_s = jnp.einsum('bhqk,bqkd->bhqd', p_s.astype(vg.dtype), vg)
        return o_c, o_s

    o_c, o_s = lax.map(one, (qb, jnp.arange(nq)))
    o_c, o_s = unblock(o_c), unblock(o_s)
    o_w = banded_attention(q[:, None], heads(kw, 1), heads(vw, 1), NSA_WINDOW, slopes[None, :], None)
    g = jax.nn.sigmoid(gl.astype(jnp.float32)).reshape(B, S, NSA_N_BRANCH, H).transpose(0, 3, 1, 2).astype(q.dtype)
    o = g[..., 0:1] * o_c + g[..., 1:2] * o_s + g[..., 2:3] * o_w
    return merge_heads(o)


def moba_mixer(z, slopes):
    d, H = HEAD_DIM, MOBA_HEADS
    q, k, v = [heads(t, H) for t in split_last(z, [H * d] * 3)]
    B, _, S, _ = q.shape
    scale = d ** -0.5
    n_blk = -(-S // MOBA_BLOCK)
    pad = n_blk * MOBA_BLOCK - S
    kp = jnp.pad(k, ((0, 0), (0, 0), (0, pad), (0, 0)))
    vp = jnp.pad(v, ((0, 0), (0, 0), (0, pad), (0, 0)))
    k_blk = kp.reshape(B, H, n_blk, MOBA_BLOCK, d)
    v_blk = vp.reshape(B, H, n_blk, MOBA_BLOCK, d)
    k_mean = jnp.mean(k_blk, axis=3)
    topk = min(MOBA_TOPK, n_blk)
    nq = S // QB
    qb = q.reshape(B, H, nq, QB, d).transpose(2, 0, 1, 3, 4)
    m = slopes.astype(jnp.float32)[None, :, None, None]
    bi = jnp.arange(B)[:, None, None, None]
    hi = jnp.arange(H)[None, :, None, None]
    blk_ids = jnp.arange(n_blk)
    ng = topk * MOBA_BLOCK

    def one(args):
        qi, n = args
        tq = n * QB + jnp.arange(QB)
        cur = (n * QB) // MOBA_BLOCK
        gate = jnp.einsum('bhqd,bhjd->bhqj', qi, k_mean).astype(jnp.float32)
        gate = jnp.where(blk_ids < cur, gate, NEG)
        _, idx = lax.top_k(gate, topk)
        valid = idx < cur
        kg = k_blk[bi, hi, idx].reshape(B, H, QB, ng, d)
        vg = v_blk[bi, hi, idx].reshape(B, H, QB, ng, d)
        key_pos = (idx[..., None] * MOBA_BLOCK + jnp.arange(MOBA_BLOCK)).reshape(B, H, QB, ng)
        s_g = jnp.einsum('bhqd,bhqkd->bhqk', qi, kg) * scale - m * (tq[:, None] - key_pos)
        mask_g = jnp.repeat(valid, MOBA_BLOCK, axis=-1)
        start = cur * MOBA_BLOCK
        k_own = lax.dynamic_slice_in_dim(kp, start, MOBA_BLOCK, axis=2)
        v_own = lax.dynamic_slice_in_dim(vp, start, MOBA_BLOCK, axis=2)
        dist_own = tq[:, None] - (start + jnp.arange(MOBA_BLOCK))[None, :]
        s_o = jnp.einsum('bhqd,bhsd->bhqs', qi, k_own) * scale - m * dist_own
        mask_o = jnp.broadcast_to(dist_own >= 0, s_o.shape)
        p = masked_softmax(jnp.concatenate([s_g, s_o], axis=-1),
                           jnp.concatenate([mask_g, mask_o], axis=-1)).astype(v.dtype)
        return (jnp.einsum('bhqk,bhqkd->bhqd', p[..., :ng], vg)
                + jnp.einsum('bhqs,bhsd->bhqd', p[..., ng:], v_own))

    return merge_heads(unblock(lax.map(one, (qb, jnp.arange(nq)))))


def setup_inputs(seed: int = 0) -> dict:
    key = jax.random.key(seed)
    ks = jax.random.split(key, 24)
    f32 = jnp.float32

    def nrm(k, shape, s):
        return jax.random.normal(k, shape, f32) * s

    return {
        'x': nrm(ks[0], (BATCH, SEQ, D_MODEL), 1.0),
        'c': nrm(ks[1], (BATCH, D_MODEL), 1.0),
        'norm_mix_g': 1.0 + nrm(ks[2], (DEPTH, D_MODEL), 0.02),
        'norm_mlp_g': 1.0 + nrm(ks[3], (DEPTH, D_MODEL), 0.02),
        'w_ada': nrm(ks[4], (DEPTH, D_MODEL, N_ADA * D_MODEL), 0.5 * D_MODEL ** -0.5),
        'b_ada': nrm(ks[5], (DEPTH, N_ADA * D_MODEL), 0.01),
        'w_in': nrm(ks[6], (DEPTH, D_MODEL, D_IN), D_MODEL ** -0.5),
        'w_out': nrm(ks[7], (DEPTH, D_MIX, D_MODEL), D_MIX ** -0.5),
        'swa_sinks': nrm(ks[8], (DEPTH, SWA_HEADS), 1.0),
        'mla_q_norm_g': 1.0 + nrm(ks[9], (DEPTH, MLA_Q_RANK), 0.02),
        'mla_kv_norm_g': 1.0 + nrm(ks[10], (DEPTH, MLA_KV_RANK), 0.02),
        'mla_w_uq': nrm(ks[11], (DEPTH, MLA_Q_RANK, MLA_HEADS * (MLA_NOPE + MLA_ROPE)), MLA_Q_RANK ** -0.5),
        'mla_w_ukv': nrm(ks[12], (DEPTH, MLA_KV_RANK, MLA_HEADS * (MLA_NOPE + MLA_V)), MLA_KV_RANK ** -0.5),
        'nsa_cmp_pos_k': nrm(ks[13], (DEPTH, NSA_CMP_LEN, HEAD_DIM), 0.02),
        'nsa_cmp_pos_v': nrm(ks[14], (DEPTH, NSA_CMP_LEN, HEAD_DIM), 0.02),
        'nsa_cmp_k_w1': nrm(ks[15], (DEPTH, NSA_CMP_LEN * HEAD_DIM, NSA_CMP_HIDDEN), (NSA_CMP_LEN * HEAD_DIM) ** -0.5),
        'nsa_cmp_k_w2': nrm(ks[16], (DEPTH, NSA_CMP_HIDDEN, HEAD_DIM), NSA_CMP_HIDDEN ** -0.5),
        'nsa_cmp_v_w1': nrm(ks[17], (DEPTH, NSA_CMP_LEN * HEAD_DIM, NSA_CMP_HIDDEN), (NSA_CMP_LEN * HEAD_DIM) ** -0.5),
        'nsa_cmp_v_w2': nrm(ks[18], (DEPTH, NSA_CMP_HIDDEN, HEAD_DIM), NSA_CMP_HIDDEN ** -0.5),
        'w_up': nrm(ks[19], (DEPTH, D_MODEL, D_FF), D_MODEL ** -0.5),
        'w_down': nrm(ks[20], (DEPTH, D_FF, D_MODEL), D_FF ** -0.5),
        'final_norm_g': 1.0 + nrm(ks[21], (D_MODEL,), 0.02),
    }


def reference(x, c, norm_mix_g, norm_mlp_g, w_ada, b_ada, w_in, w_out, swa_sinks,
              mla_q_norm_g, mla_kv_norm_g, mla_w_uq, mla_w_ukv,
              nsa_cmp_pos_k, nsa_cmp_pos_v, nsa_cmp_k_w1, nsa_cmp_k_w2, nsa_cmp_v_w1, nsa_cmp_v_w2,
              w_up, w_down, final_norm_g):
    slopes_a, slopes_c, slopes_d = alibi_slopes()
    c_act = jax.nn.silu(c)
    for l in range(DEPTH):
        mod = c_act @ w_ada[l] + b_ada[l]
        sh1, sc1, g1, sh2, sc2, g2 = [t[:, None, :] for t in jnp.split(mod, N_ADA, axis=-1)]
        h = rmsnorm(x, norm_mix_g[l]) * (1.0 + sc1) + sh1
        z = h @ w_in[l]
        za, zb, zc, zd = split_last(z, [SWA_COLS, MLA_COLS, NSA_COLS, MOBA_COLS])
        o = jnp.concatenate([
            swa_mixer(za, swa_sinks[l], slopes_a),
            mla_mixer(zb, mla_q_norm_g[l], mla_kv_norm_g[l], mla_w_uq[l], mla_w_ukv[l]),
            nsa_mixer(zc, slopes_c, nsa_cmp_pos_k[l], nsa_cmp_pos_v[l],
                      nsa_cmp_k_w1[l], nsa_cmp_k_w2[l], nsa_cmp_v_w1[l], nsa_cmp_v_w2[l]),
            moba_mixer(zd, slopes_d),
        ], axis=-1)
        x = x + g1 * (o @ w_out[l])
        h = rmsnorm(x, norm_mlp_g[l]) * (1.0 + sc2) + sh2
        x = x + g2 * (jnp.square(jax.nn.relu(h @ w_up[l])) @ w_down[l])
    return rmsnorm(x, final_norm_g)
```

```python
import functools
import math

import numpy as np
import jax
import jax.numpy as jnp
from jax import lax
from jax.experimental import pallas as pl
from jax.experimental.pallas import tpu as pltpu

F32 = jnp.float32
BF16 = jnp.bfloat16

D_MODEL = 1024
DEPTH = 2
HEAD_DIM = 64
LANES = 128
QB = 128
NEG = -1e30
TINY = 1e-30
BIG = 1e9
RMS_EPS = 1e-6
N_ADA = 6
N_HEADS = 4

SWA_KV_HEADS = 2
SWA_WINDOW = 128
MLA_Q_RANK = 192
MLA_KV_RANK = 128
MLA_NOPE = 64
MLA_ROPE = 32
ROPE_THETA = 10000.0
NSA_CMP_LEN = 32
NSA_CMP_STRIDE = 16
NSA_CMP_HIDDEN = 128
NSA_SEL_BLOCK = 64
NSA_TOPN = 16
NSA_WINDOW = 512
NSA_N_BRANCH = 3
MOBA_BLOCK = 256
MOBA_TOPK = 3
D_FF = 4 * D_MODEL
N_ALIBI = 3 * N_HEADS

_S_ALL = [2.0 ** (-8.0 * (i + 1) / N_ALIBI) for i in range(N_ALIBI)]
SLOPES_A = [float(np.float32(v)) for v in _S_ALL[0::3]]
SLOPES_C = [float(np.float32(v)) for v in _S_ALL[1::3]]
SLOPES_D = [float(np.float32(v)) for v in _S_ALL[2::3]]

VMEM_LIMIT = 56 * 1024 * 1024

_NT = (((1,), (1,)), ((), ()))


def _cparams(sem):
    return pltpu.CompilerParams(dimension_semantics=sem, vmem_limit_bytes=VMEM_LIMIT)


def _split3(a):
    hi = a.astype(BF16)
    r1 = a - hi.astype(F32)
    mid = r1.astype(BF16)
    lo = (r1 - mid.astype(F32)).astype(BF16)
    return hi, mid, lo


def _dot(a, b):
    return jnp.dot(a, b, preferred_element_type=F32)


def _dot_nt(a, b):
    return lax.dot_general(a, b, _NT, preferred_element_type=F32)


def _ada_kernel(c_ref, w_ref, b_ref, o_ref):
    c = c_ref[...]
    a = c * jax.nn.sigmoid(c)
    w = w_ref[0]
    a_hi = a.astype(BF16)
    a_lo = (a - a_hi.astype(F32)).astype(BF16)
    w_hi = w.astype(BF16)
    w_lo = (w - w_hi.astype(F32)).astype(BF16)
    o_ref[0] = _dot(a_hi, w_hi) + _dot(a_hi, w_lo) + _dot(a_lo, w_hi) + b_ref[0]


def _ada(c8, w_ada, b_ada):
    depth, d, n = w_ada.shape
    nb = n // d
    return pl.pallas_call(
        _ada_kernel,
        grid=(depth, nb),
        in_specs=[
            pl.BlockSpec((8, d), lambda l, j: (0, 0)),
            pl.BlockSpec((1, d, d), lambda l, j: (l, 0, j)),
            pl.BlockSpec((1, 1, d), lambda l, j: (l, 0, j)),
        ],
        out_specs=pl.BlockSpec((1, 8, d), lambda l, j: (l, 0, j)),
        out_shape=jax.ShapeDtypeStruct((depth, 8, n), F32),
        compiler_params=_cparams(("parallel", "parallel")),
        name="ada_mod",
    )(c8, w_ada, b_ada.reshape(depth, 1, n))


_IN_GROUPS = (("mla", 640), ("qA", 512), ("kA", 256), ("vA", 256), ("qC", 512), ("kcvc", 128),
              ("kv4", 512), ("gl", 128), ("qD", 512), ("kD", 512), ("vD", 512))
_IN_OFFS = {}
_o = 0
for _n, _w in _IN_GROUPS:
    _IN_OFFS[_n] = (_o, _o + _w)
    _o += _w
IN_COLS = _o


def _in_proj_kernel(x_ref, mod_ref, g_ref, w_ref, qg_ref, kvg_ref, wq_ref, wkv_ref, cos_ref, sin_ref,
                    qB_ref, kB_ref, vB_ref, qA_ref, kA_ref, vA_ref, qC_ref, kcvc_ref, kv4_ref, gl_ref,
                    qD_ref, kD_ref, vD_ref):
    x = x_ref[0]
    y = x * lax.rsqrt(jnp.mean(x * x, axis=-1, keepdims=True) + RMS_EPS) * g_ref[...]
    h = y * (1.0 + mod_ref[0, 1:2, :]) + mod_ref[0, 0:1, :]
    hb = h.astype(BF16)

    def proj(name):
        a, b = _IN_OFFS[name]
        return _dot(hb, w_ref[:, a:b])

    for name, ref in (("qA", qA_ref), ("kA", kA_ref), ("vA", vA_ref), ("qC", qC_ref), ("kcvc", kcvc_ref),
                      ("kv4", kv4_ref), ("gl", gl_ref), ("qD", qD_ref), ("kD", kD_ref), ("vD", vD_ref)):
        ref[0] = proj(name).astype(ref.dtype)

    zb = proj("mla")
    cq = zb[:, 0:256]
    ckv = zb[:, 256:384]
    kpm = zb[:, 384:512]
    kpr = zb[:, 512:640]
    cqn = cq * lax.rsqrt(jnp.sum(cq * cq, axis=-1, keepdims=True) * (1.0 / MLA_Q_RANK) + RMS_EPS) * qg_ref[...]
    ckvn = ckv * lax.rsqrt(jnp.mean(ckv * ckv, axis=-1, keepdims=True) + RMS_EPS) * kvg_ref[...]
    qall = _dot(cqn.astype(BF16), wq_ref[...])
    kvall = _dot(ckvn.astype(BF16), wkv_ref[...])
    cosq = cos_ref[...]
    sinq = sin_ref[...]
    kpe = kpm * cosq + kpr * sinq
    for hd in range(N_HEADS):
        lo = hd * 2 * LANES
        sl = slice(hd * LANES, (hd + 1) * LANES)
        qB_ref[0, :, sl] = (qall[:, lo:lo + LANES] * cosq + qall[:, lo + LANES:lo + 2 * LANES] * sinq).astype(BF16)
        kB_ref[0, :, sl] = (kvall[:, lo:lo + LANES] + kpe).astype(BF16)
        vB_ref[0, :, sl] = kvall[:, lo + LANES:lo + 2 * LANES].astype(BF16)


def _in_proj(x, mod, g, wmain, qg, kvg, wq, wkv, cos_t, sin_t, ts=256):
    b, s, d = x.shape
    nt = s // ts
    tok = lambda w: pl.BlockSpec((1, ts, w), lambda i, j: (i, j, 0))
    full = lambda a: pl.BlockSpec(a.shape, lambda i, j: (0,) * a.ndim)
    outs = [("qB", 512, BF16), ("kB", 512, BF16), ("vB", 512, BF16), ("qA", 512, BF16), ("kA", 256, BF16),
            ("vA", 256, BF16), ("qC", 512, BF16), ("kcvc", 128, BF16), ("kv4", 512, BF16), ("gl", 128, F32),
            ("qD", 512, BF16), ("kD", 512, BF16), ("vD", 512, BF16)]
    res = pl.pallas_call(
        _in_proj_kernel,
        grid=(b, nt),
        in_specs=[tok(d), pl.BlockSpec((1, N_ADA, d), lambda i, j: (i, 0, 0)), full(g), full(wmain), full(qg),
                  full(kvg), full(wq), full(wkv),
                  pl.BlockSpec((ts, LANES), lambda i, j: (j, 0)), pl.BlockSpec((ts, LANES), lambda i, j: (j, 0))],
        out_specs=[tok(w) for _, w, _ in outs],
        out_shape=[jax.ShapeDtypeStruct((b, s, w), dt) for _, w, dt in outs],
        compiler_params=_cparams(("parallel", "parallel")),
        name="in_proj",
    )(x, mod, g, wmain, qg, kvg, wq, wkv, cos_t, sin_t)
    return {n: r for (n, _, _), r in zip(outs, res)}


def _flash_step(q, k, v, bias, m, l, acc):
    s = _dot_nt(q, k) + bias
    m_new = jnp.maximum(m, jnp.max(s, axis=1, keepdims=True))
    alpha = jnp.exp(m - m_new)
    p = jnp.exp(s - m_new)
    l = alpha * l + jnp.sum(p, axis=1, keepdims=True)
    acc = alpha * acc + _dot(p.astype(BF16), v)
    return m_new, l, acc


def _dist(tq, tk, t0, k0):
    r = lax.broadcasted_iota(jnp.int32, (tq, tk), 0)
    c = lax.broadcasted_iota(jnp.int32, (tq, tk), 1)
    return (r - c) + (t0 - k0)


def _pack_heads(os):
    return jnp.concatenate([o[:, :HEAD_DIM] for o in os], axis=1)


def _swa_kernel(sink_ref, q_ref, kp_ref, kc_ref, vp_ref, vc_ref, o_ref):
    n = pl.program_id(1)
    dist_c = _dist(QB, QB, 0, 0)
    dist_p = dist_c + QB
    ok_c = dist_c >= 0
    ok_p = dist_p < jnp.where(n > 0, SWA_WINDOW, 0)
    dc = dist_c.astype(F32)
    dp = dist_p.astype(F32)
    outs = []
    for hd in range(N_HEADS):
        kv = hd // (N_HEADS // SWA_KV_HEADS)
        q = q_ref[0, :, hd * LANES:(hd + 1) * LANES]
        ksl = slice(kv * LANES, (kv + 1) * LANES)
        slope = SLOPES_A[hd]
        sink = sink_ref[hd]
        s_c = jnp.where(ok_c, _dot_nt(q, kc_ref[0, :, ksl]) - slope * dc, NEG)
        s_p = jnp.where(ok_p, _dot_nt(q, kp_ref[0, :, ksl]) - slope * dp, NEG)
        m = jnp.maximum(jnp.maximum(jnp.max(s_c, axis=1, keepdims=True), jnp.max(s_p, axis=1, keepdims=True)), sink)
        e_c = jnp.where(ok_c, jnp.exp(s_c - m), 0.0)
        e_p = jnp.where(ok_p, jnp.exp(s_p - m), 0.0)
        den = jnp.sum(e_c, axis=1, keepdims=True) + jnp.sum(e_p, axis=1, keepdims=True) + jnp.exp(sink - m)
        inv = 1.0 / jnp.maximum(den, TINY)
        o = _dot((e_c * inv).astype(BF16), vc_ref[0, :, ksl]) + _dot((e_p * inv).astype(BF16), vp_ref[0, :, ksl])
        outs.append(o)
    o_ref[0] = _pack_heads(outs).astype(o_ref.dtype)


def _swa(qA, kA, vA, sinks):
    b, s, _ = qA.shape
    nq = s // QB
    cur = lambda w: pl.BlockSpec((1, QB, w), lambda i, j: (i, j, 0))
    prev = lambda w: pl.BlockSpec((1, QB, w), lambda i, j: (i, jnp.maximum(j - 1, 0), 0))
    return pl.pallas_call(
        _swa_kernel,
        grid=(b, nq),
        in_specs=[pl.BlockSpec(memory_space=pltpu.SMEM), cur(512), prev(256), cur(256), prev(256), cur(256)],
        out_specs=cur(N_HEADS * HEAD_DIM),
        out_shape=jax.ShapeDtypeStruct((b, s, N_HEADS * HEAD_DIM), BF16),
        compiler_params=_cparams(("parallel", "parallel")),
        name="swa",
    )(sinks, qA, kA, kA, vA, vA)


MLA_SCALE = float((MLA_NOPE + MLA_ROPE) ** -0.5)


def _mla_kernel(q_ref, k_ref, v_ref, o_ref, m_sc, l_sc, acc_sc, *, tq, tk):
    i = pl.program_id(2)
    j = pl.program_id(3)
    nk = pl.num_programs(3)
    last = ((i + 1) * tq - 1) // tk

    @pl.when(j == 0)
    def _():
        m_sc[...] = jnp.full(m_sc.shape, NEG, F32)
        l_sc[...] = jnp.zeros(l_sc.shape, F32)
        acc_sc[...] = jnp.zeros(acc_sc.shape, F32)

    @pl.when(j <= last)
    def _():
        bias = jnp.where(_dist(tq, tk, i * tq, j * tk) >= 0, 0.0, NEG)
        for hd in range(2):
            sl = slice(hd * LANES, (hd + 1) * LANES)
            s = _dot_nt(q_ref[0, :, sl], k_ref[0, :, sl]) * MLA_SCALE + bias
            m = m_sc[hd]
            m_new = jnp.maximum(m, jnp.max(s, axis=1, keepdims=True))
            alpha = jnp.exp(m - m_new)
            p = jnp.exp(s - m_new)
            l_sc[hd] = alpha * l_sc[hd] + jnp.sum(p, axis=1, keepdims=True)
            acc_sc[hd] = alpha * acc_sc[hd] + _dot(p.astype(BF16), v_ref[0, :, sl])
            m_sc[hd] = m_new

    @pl.when(j == nk - 1)
    def _():
        outs = [acc_sc[hd] * (1.0 / jnp.maximum(l_sc[hd], TINY)) for hd in range(2)]
        o_ref[0] = _pack_heads(outs).astype(o_ref.dtype)


def _mla(qB, kB, vB, tq=256, tk=256):
    b, s, _ = qB.shape
    nq, nk = s // tq, s // tk

    def kv_map(bi, p, i, j):
        return (bi, jnp.minimum(j, ((i + 1) * tq - 1) // tk), p)

    return pl.pallas_call(
        functools.partial(_mla_kernel, tq=tq, tk=tk),
        grid=(b, 2, nq, nk),
        in_specs=[pl.BlockSpec((1, tq, 2 * LANES), lambda bi, p, i, j: (bi, i, p)),
                  pl.BlockSpec((1, tk, 2 * LANES), kv_map),
                  pl.BlockSpec((1, tk, 2 * LANES), kv_map)],
        out_specs=pl.BlockSpec((1, tq, LANES), lambda bi, p, i, j: (bi, i, p)),
        out_shape=jax.ShapeDtypeStruct((b, s, N_HEADS * HEAD_DIM), BF16),
        scratch_shapes=[pltpu.VMEM((2, tq, 1), F32), pltpu.VMEM((2, tq, 1), F32), pltpu.VMEM((2, tq, LANES), F32)],
        compiler_params=_cparams(("parallel", "parallel", "parallel", "arbitrary")),
        name="mla_attn",
    )(qB, kB, vB)


def _gelu_tanh(x):
    return x * (0.5 * (1.0 + jnp.tanh(math.sqrt(2.0 / math.pi) * (x + 0.044715 * (x * x * x)))))


def _nsa_cmp_kernel(ch_ref, w1_ref, pos_ref, w1f_ref, w2_ref, kc_ref, vc_ref):
    nc = ch_ref.shape[1]
    u = _dot(ch_ref[0], w1_ref[...])
    for idx, out in ((0, kc_ref), (1, vc_ref)):
        pos = jnp.broadcast_to(pos_ref[idx], (8, pos_ref.shape[2]))
        w1f = w1f_ref[idx]
        p_hi = pos.astype(BF16)
        p_lo = (pos - p_hi.astype(F32)).astype(BF16)
        w_hi = w1f.astype(BF16)
        w_lo = (w1f - w_hi.astype(F32)).astype(BF16)
        posb = (_dot(p_hi, w_hi) + _dot(p_hi, w_lo) + _dot(p_lo, w_hi))[0:1, :]
        top = u[:, (2 * idx) * LANES:(2 * idx + 1) * LANES]
        bot = u[:, (2 * idx + 1) * LANES:(2 * idx + 2) * LANES]
        hid = top + pltpu.roll(bot, nc - 1, 0) + posb
        out[0] = _dot(_gelu_tanh(hid).astype(BF16), w2_ref[idx]).astype(out.dtype)


def _nsa_compress(ch, w1aug, pos2, w1f, w2p):
    b, nc, w = ch.shape
    full = lambda a: pl.BlockSpec(a.shape, lambda i: (0,) * a.ndim)
    return pl.pallas_call(
        _nsa_cmp_kernel,
        grid=(b,),
        in_specs=[pl.BlockSpec((1, nc, w), lambda i: (i, 0, 0)), full(w1aug), full(pos2), full(w1f), full(w2p)],
        out_specs=[pl.BlockSpec((1, nc, LANES), lambda i: (i, 0, 0))] * 2,
        out_shape=[jax.ShapeDtypeStruct((b, nc, LANES), BF16)] * 2,
        compiler_params=_cparams(("parallel",)),
        name="nsa_compress",
    )(ch, w1aug, pos2, w1f, w2p)


NSA_CHUNK = 256


def _nsa_kernel(q_ref, kc_ref, vc_ref, kv_ref, gl_ref, e_ref, mm_ref, o_ref, imp_sc):
    n = pl.program_id(1)
    t0 = n * QB
    nc = kc_ref.shape[1]
    nbl = LANES
    qs = [q_ref[0, :, hd * LANES:(hd + 1) * LANES] for hd in range(N_HEADS)]

    r = lax.broadcasted_iota(jnp.int32, (QB, nc), 0) + t0
    cidx = lax.broadcasted_iota(jnp.int32, (QB, nc), 1)
    dist_c = r - (cidx * NSA_CMP_STRIDE + (NSA_CMP_LEN - 1))
    ok = jnp.logical_and(dist_c >= 0, cidx < nc - 1)
    dcf = dist_c.astype(F32)
    kc = kc_ref[0]
    vc = vc_ref[0]
    psum = None
    o_cmp = []
    for hd in range(N_HEADS):
        s = jnp.where(ok, _dot_nt(qs[hd], kc) - SLOPES_C[hd] * dcf, NEG)
        m = jnp.max(s, axis=1, keepdims=True)
        e = jnp.where(ok, jnp.exp(s - m), 0.0)
        p = e * (1.0 / jnp.maximum(jnp.sum(e, axis=1, keepdims=True), TINY))
        psum = p if psum is None else psum + p
        o_cmp.append(_dot(p.astype(BF16), vc))

    mm = mm_ref[...]
    imp = None
    for piece in _split3(psum):
        d = _dot(piece, mm)
        imp = d if imp is None else imp + d
    imp_t = imp.T
    jrow = lax.broadcasted_iota(jnp.int32, (nbl, QB), 0)
    tcol = lax.broadcasted_iota(jnp.int32, (nbl, QB), 1) + t0
    cur = tcol >> 6
    cand = jrow <= cur
    forced = jnp.logical_or(jnp.logical_or(jrow == 0, jrow == cur), jrow == cur - 1)
    a = jnp.where(cand, jnp.where(forced, BIG, imp_t), NEG)
    imp_sc[...] = a
    n_cand = (t0 + QB - 1) // NSA_SEL_BLOCK + 1

    def rank_body(i, cnt):
        row = imp_sc[pl.ds(i, 1), :]
        c_ge = jnp.where(row >= a, 1.0, 0.0)
        c_gt = jnp.where(row > a, 1.0, 0.0)
        return cnt + jnp.where(jrow > i, c_ge, c_gt)

    rank = lax.fori_loop(0, n_cand, rank_body, jnp.zeros((nbl, QB), F32))
    sel_t = jnp.where(jnp.logical_and(rank < NSA_TOPN, cand), 0.0, NEG)
    selneg = sel_t.T.astype(BF16)

    own = t0 // NSA_CHUNK
    neg_slopes = [-v for v in SLOPES_C]

    def sel_step(c, carry, causal):
        k0 = pl.multiple_of(c * NSA_CHUNK, NSA_CHUNK)
        kblk = kv_ref[0, pl.ds(k0, NSA_CHUNK), 0:LANES]
        vblk = kv_ref[0, pl.ds(k0, NSA_CHUNK), LANES:2 * LANES]
        mb = _dot_nt(selneg, e_ref[pl.ds(k0, NSA_CHUNK), :])
        dist = _dist(QB, NSA_CHUNK, t0, k0)
        if causal:
            mb = jnp.where(dist >= 0, mb, NEG)
        df = dist.astype(F32)
        out = []
        for hd in range(N_HEADS):
            m, l, acc = carry[hd]
            out.append(_flash_step(qs[hd], kblk, vblk, df * neg_slopes[hd] + mb, m, l, acc))
        return tuple(out)

    init = tuple((jnp.full((QB, 1), NEG, F32), jnp.zeros((QB, 1), F32), jnp.zeros((QB, LANES), F32))
                 for _ in range(N_HEADS))
    carry = sel_step(own, init, True)
    carry = lax.fori_loop(0, own, lambda c, cr: sel_step(c, cr, False), carry)
    o_sel = [acc * (1.0 / jnp.maximum(l, TINY)) for (_, l, acc) in carry]

    nwb = NSA_WINDOW // QB

    def win_step(kb, carry, own_blk):
        k0 = pl.multiple_of(kb * QB, QB)
        kblk = kv_ref[0, pl.ds(k0, QB), 2 * LANES:3 * LANES]
        vblk = kv_ref[0, pl.ds(k0, QB), 3 * LANES:4 * LANES]
        dist = _dist(QB, QB, t0, k0)
        okw = (dist >= 0) if own_blk else (dist < NSA_WINDOW)
        df = dist.astype(F32)
        out = []
        for hd in range(N_HEADS):
            m, l, acc = carry[hd]
            out.append(_flash_step(qs[hd], kblk, vblk, jnp.where(okw, df * neg_slopes[hd], NEG), m, l, acc))
        return tuple(out)

    carry = win_step(n, init, True)
    carry = lax.fori_loop(jnp.maximum(n - nwb, 0), n, lambda kb, cr: win_step(kb, cr, False), carry)
    o_win = [acc * (1.0 / jnp.maximum(l, TINY)) for (_, l, acc) in carry]

    g = jax.nn.sigmoid(gl_ref[0])
    outs = []
    for hd in range(N_HEADS):
        outs.append(g[:, hd:hd + 1] * o_cmp[hd] + g[:, N_HEADS + hd:N_HEADS + hd + 1] * o_sel[hd]
                    + g[:, 2 * N_HEADS + hd:2 * N_HEADS + hd + 1] * o_win[hd])
    o_ref[0] = _pack_heads(outs).astype(o_ref.dtype)


def _nsa(qC, kcmp, vcmp, kv4, gl, e_mat, m_mat):
    b, s, _ = qC.shape
    nq = s // QB
    nc = kcmp.shape[1]
    full = lambda a: pl.BlockSpec(a.shape, lambda i, j: (0,) * a.ndim)
    return pl.pallas_call(
        _nsa_kernel,
        grid=(b, nq),
        in_specs=[pl.BlockSpec((1, QB, 512), lambda i, j: (i, j, 0)),
                  pl.BlockSpec((1, nc, LANES), lambda i, j: (i, 0, 0)),
                  pl.BlockSpec((1, nc, LANES), lambda i, j: (i, 0, 0)),
                  pl.BlockSpec((1, s, 512), lambda i, j: (i, 0, 0)),
                  pl.BlockSpec((1, QB, LANES), lambda i, j: (i, j, 0)),
                  full(e_mat), full(m_mat)],
        out_specs=pl.BlockSpec((1, QB, N_HEADS * HEAD_DIM), lambda i, j: (i, j, 0)),
        out_shape=jax.ShapeDtypeStruct((b, s, N_HEADS * HEAD_DIM), BF16),
        scratch_shapes=[pltpu.VMEM((LANES, QB), F32)],
        compiler_params=_cparams(("parallel", "arbitrary")),
        name="nsa_attn",
    )(qC, kcmp, vcmp, kv4, gl, e_mat, m_mat)


def _moba_kernel(q_ref, k_ref, v_ref, o_ref, kmean_sc, gate_sc, *, pair):
    p = pl.program_id(1)
    n = pl.program_id(2)
    t0 = n * QB
    s_len = k_ref.shape[1]
    n_blk = s_len // MOBA_BLOCK
    nbr = max(8, n_blk)
    cur = t0 // MOBA_BLOCK

    @pl.when(n == 0)
    def _():
        kmean_sc[...] = jnp.zeros(kmean_sc.shape, F32)
        for hd in range(2):
            for jb in range(n_blk):
                blk = k_ref[0, jb * MOBA_BLOCK:(jb + 1) * MOBA_BLOCK, hd * LANES:(hd + 1) * LANES].astype(F32)
                kmean_sc[hd, jb:jb + 1, :] = jnp.mean(blk, axis=0, keepdims=True)

    jrow = lax.broadcasted_iota(jnp.int32, (nbr, QB), 0)
    lrow = lax.broadcasted_iota(jnp.int32, (LANES, LANES), 0)
    outs = []
    for hd in range(2):
        sl = slice(hd * LANES, (hd + 1) * LANES)
        q = q_ref[0, :, sl]
        slope = jnp.where(p == 0, SLOPES_D[hd], SLOPES_D[2 + hd]) if pair else SLOPES_D[hd]
        gate_t = _dot_nt(kmean_sc[hd].astype(BF16), q)[0:nbr, :]
        a = jnp.where(jrow < cur, gate_t, NEG)
        gate_sc[...] = a

        def rank_body(i, cnt):
            row = gate_sc[pl.ds(i, 1), :]
            c_ge = jnp.where(row >= a, 1.0, 0.0)
            c_gt = jnp.where(row > a, 1.0, 0.0)
            return cnt + jnp.where(jrow > i, c_ge, c_gt)

        rank = lax.fori_loop(0, cur, rank_body, jnp.zeros((nbr, QB), F32))
        sel_t = jnp.where(jnp.logical_and(rank < MOBA_TOPK, jrow < cur), 0.0, NEG)
        if nbr < LANES:
            sel_t = jnp.concatenate([sel_t, jnp.full((LANES - nbr, QB), NEG, F32)], axis=0)
        selneg = sel_t.T.astype(BF16)

        def step(jb, carry, own_blk):
            k0 = pl.multiple_of(jb * MOBA_BLOCK, MOBA_BLOCK)
            kblk = k_ref[0, pl.ds(k0, MOBA_BLOCK), sl]
            vblk = v_ref[0, pl.ds(k0, MOBA_BLOCK), sl]
            dist = _dist(QB, MOBA_BLOCK, t0, k0)
            bias = dist.astype(F32) * (-slope)
            if own_blk:
                bias = jnp.where(dist >= 0, bias, NEG)
            else:
                onehot = jnp.where(lrow == jb, 1.0, 0.0).astype(BF16)
                mb = _dot(selneg, onehot)
                bias = bias + jnp.concatenate([mb] * (MOBA_BLOCK // LANES), axis=1)
            m, l, acc = carry
            return _flash_step(q, kblk, vblk, bias, m, l, acc)

        init = (jnp.full((QB, 1), NEG, F32), jnp.zeros((QB, 1), F32), jnp.zeros((QB, LANES), F32))
        carry = step(cur, init, True)
        carry = lax.fori_loop(0, cur, lambda jb, cr: step(jb, cr, False), carry)
        outs.append(carry[2] * (1.0 / jnp.maximum(carry[1], TINY)))
    o_ref[0] = _pack_heads(outs).astype(o_ref.dtype)


def _moba(qD, kD, vD):
    b, s, _ = qD.shape
    nq = s // QB
    nbr = max(8, s // MOBA_BLOCK)
    return pl.pallas_call(
        functools.partial(_moba_kernel, pair=True),
        grid=(b, 2, nq),
        in_specs=[pl.BlockSpec((1, QB, 2 * LANES), lambda i, p, j: (i, j, p)),
                  pl.BlockSpec((1, s, 2 * LANES), lambda i, p, j: (i, 0, p)),
                  pl.BlockSpec((1, s, 2 * LANES), lambda i, p, j: (i, 0, p))],
        out_specs=pl.BlockSpec((1, QB, LANES), lambda i, p, j: (i, j, p)),
        out_shape=jax.ShapeDtypeStruct((b, s, N_HEADS * HEAD_DIM), BF16),
        scratch_shapes=[pltpu.VMEM((2, LANES, LANES), F32), pltpu.VMEM((nbr, QB), F32)],
        compiler_params=_cparams(("parallel", "parallel", "arbitrary")),
        name="moba_attn",
    )(qD, kD, vD)


def _post_kernel(oa_ref, ob_ref, oc_ref, od_ref, x_ref, mod_ref, wo_ref, g_ref, wu_ref, wd_ref, fg_ref, out_ref,
                 *, final, ffc):
    x = x_ref[0]
    mix = None
    for idx, ref in enumerate((oa_ref, ob_ref, oc_ref, od_ref)):
        d = _dot(ref[0], wo_ref[idx])
        mix = d if mix is None else mix + d
    x1 = x + mod_ref[0, 2:3, :] * mix
    y = x1 * lax.rsqrt(jnp.mean(x1 * x1, axis=-1, keepdims=True) + RMS_EPS) * g_ref[...]
    hb = (y * (1.0 + mod_ref[0, 4:5, :]) + mod_ref[0, 3:4, :]).astype(BF16)
    acc = None
    for c in range(wu_ref.shape[1] // ffc):
        hid = jnp.maximum(_dot(hb, wu_ref[:, c * ffc:(c + 1) * ffc]), 0.0)
        d = _dot((hid * hid).astype(BF16), wd_ref[c * ffc:(c + 1) * ffc, :])
        acc = d if acc is None else acc + d
    x2 = x1 + mod_ref[0, 5:6, :] * acc
    if final:
        x2 = x2 * lax.rsqrt(jnp.mean(x2 * x2, axis=-1, keepdims=True) + RMS_EPS) * fg_ref[...]
    out_ref[0] = x2


def _post(oa, ob, oc, od, x, mod, wo4, g, wu, wd, fg, final, ts=256, ffc=1024):
    b, s, d = x.shape
    nt = s // ts
    tok = lambda w: pl.BlockSpec((1, ts, w), lambda i, j: (i, j, 0))
    full = lambda a: pl.BlockSpec(a.shape, lambda i, j: (0,) * a.ndim, pipeline_mode=pl.Buffered(1))
    hw = N_HEADS * HEAD_DIM
    return pl.pallas_call(
        functools.partial(_post_kernel, final=final, ffc=ffc),
        grid=(b, nt),
        in_specs=[tok(hw), tok(hw), tok(hw), tok(hw), tok(d), pl.BlockSpec((1, N_ADA, d), lambda i, j: (i, 0, 0)),
                  full(wo4), full(g), full(wu), full(wd), full(fg)],
        out_specs=tok(d),
        out_shape=jax.ShapeDtypeStruct((b, s, d), F32),
        compiler_params=_cparams(("parallel", "parallel")),
        name="post_mlp",
    )(oa, ob, oc, od, x, mod, wo4, g, wu, wd, fg)


def _pad_heads(w, n_heads, scale=None):
    k = w.shape[0]
    w = w.reshape(k, n_heads, HEAD_DIM)
    if scale is not None:
        w = w * scale
    return jnp.pad(w, ((0, 0), (0, 0), (0, LANES - HEAD_DIM))).reshape(k, n_heads * LANES)


def _rot_half_cols(w):
    half = w.shape[1] // 2
    return jnp.concatenate([-w[:, half:], w[:, :half]], axis=1)


def _layer_weights(w_in, mla_qg, mla_kvg, w_uq, w_ukv, pos_k, pos_v, ck_w1, ck_w2, cv_w1, cv_w2):
    d = w_in.shape[0]
    z = lambda n: jnp.zeros((d, n), F32)
    o = 0

    def take(n):
        nonlocal o
        r = w_in[:, o:o + n]
        o += n
        return r

    qa, ka, va = take(256), take(128), take(128)
    cq, ckv, kpe = take(MLA_Q_RANK), take(MLA_KV_RANK), take(MLA_ROPE)
    qc, kc, vc, ks, vs, kw, vw, gl = take(256), take(64), take(64), take(64), take(64), take(64), take(64), take(12)
    qd, kd, vd = take(256), take(256), take(256)
    sc = 1.0 / 8.0
    kpe_main = jnp.concatenate([z(64), kpe, z(32)], axis=1)
    kpe_rot = jnp.concatenate([z(64), _rot_half_cols(kpe), z(32)], axis=1)
    p64 = lambda w: jnp.pad(w, ((0, 0), (0, LANES - w.shape[1])))
    groups = {
        "mla": jnp.concatenate([cq, z(64), ckv, kpe_main, kpe_rot], axis=1),
        "qA": _pad_heads(qa, 4, sc), "kA": _pad_heads(ka, 2), "vA": _pad_heads(va, 2),
        "qC": _pad_heads(qc, 4, sc), "kcvc": jnp.concatenate([kc, vc], axis=1),
        "kv4": jnp.concatenate([p64(ks), p64(vs), p64(kw), p64(vw)], axis=1), "gl": p64(gl),
        "qD": _pad_heads(qd, 4, sc), "kD": _pad_heads(kd, 4), "vD": _pad_heads(vd, 4),
    }
    wmain = jnp.concatenate([groups[n] for n, _ in _IN_GROUPS], axis=1).astype(BF16)

    dq = MLA_NOPE + MLA_ROPE
    wq_cols, wkv_cols = [], []
    for hd in range(N_HEADS):
        wh = w_uq[:, hd * dq:(hd + 1) * dq]
        zq = lambda n: jnp.zeros((MLA_Q_RANK, n), F32)
        wq_cols += [wh, zq(32), zq(64), _rot_half_cols(wh[:, MLA_NOPE:]), zq(32)]
        wk = w_ukv[:, hd * 128:(hd + 1) * 128]
        zk = jnp.zeros((MLA_KV_RANK, 64), F32)
        wkv_cols += [wk[:, :64], zk, wk[:, 64:], zk]
    wq = jnp.pad(jnp.concatenate(wq_cols, axis=1), ((0, 256 - MLA_Q_RANK), (0, 0))).astype(BF16)
    wkv = jnp.concatenate(wkv_cols, axis=1).astype(BF16)
    qg = jnp.pad(mla_qg, (0, 256 - MLA_Q_RANK)).reshape(1, 256)
    kvg = mla_kvg.reshape(1, MLA_KV_RANK)


    def aug(w1, is_v):
        w = w1.reshape(NSA_CMP_LEN, HEAD_DIM, NSA_CMP_HIDDEN)
        zz = jnp.zeros_like(w)
        w = jnp.concatenate([zz, w] if is_v else [w, zz], axis=1)
        w = w.reshape(NSA_CMP_LEN * LANES, NSA_CMP_HIDDEN)
        return w[:NSA_CMP_STRIDE * LANES], w[NSA_CMP_STRIDE * LANES:]

    kt, kb = aug(ck_w1, False)
    vt, vb = aug(cv_w1, True)
    w1aug = jnp.concatenate([kt, kb, vt, vb], axis=1).astype(BF16)
    pos2 = jnp.stack([pos_k.reshape(1, -1), pos_v.reshape(1, -1)])
    w1f = jnp.stack([ck_w1, cv_w1])
    w2p = jnp.stack([p64(ck_w2), p64(cv_w2)]).astype(BF16)
    return wmain, qg, kvg, wq, wkv, w1aug, pos2, w1f, w2p


def _rope_tables(s):
    half = MLA_ROPE // 2
    freqs = ROPE_THETA ** (-jnp.arange(half, dtype=F32) / half)
    ang = jnp.arange(s, dtype=F32)[:, None] * freqs[None, :]
    cos, sin = jnp.cos(ang), jnp.sin(ang)
    ones, zeros = jnp.ones((s, MLA_NOPE), F32), jnp.zeros((s, MLA_NOPE), F32)
    tail = jnp.zeros((s, LANES - MLA_NOPE - MLA_ROPE), F32)
    return (jnp.concatenate([ones, cos, cos, tail], axis=1), jnp.concatenate([zeros, sin, sin, tail], axis=1))


def _nsa_constants(s):
    n_sel = s // NSA_SEL_BLOCK
    nc = s // NSA_CMP_STRIDE
    assert n_sel <= LANES
    e = ((np.arange(s)[:, None] // NSA_SEL_BLOCK) == np.arange(LANES)[None, :]).astype(np.float32)
    c = np.arange(nc)[:, None]
    j = np.arange(LANES)[None, :]
    m = ((c >= 4 * j - 1) & (c <= 4 * j + 3) & (c < nc - 1) & (j < n_sel)).astype(np.float32)
    return jnp.asarray(e, BF16), jnp.asarray(m, BF16)


def kernel(x, c, norm_mix_g, norm_mlp_g, w_ada, b_ada, w_in, w_out, swa_sinks, mla_q_norm_g, mla_kv_norm_g,
           mla_w_uq, mla_w_ukv, nsa_cmp_pos_k, nsa_cmp_pos_v, nsa_cmp_k_w1, nsa_cmp_k_w2, nsa_cmp_v_w1,
           nsa_cmp_v_w2, w_up, w_down, final_norm_g):
    b, s, d = x.shape
    depth = w_in.shape[0]
    c8 = jnp.pad(c, ((0, 8 - b), (0, 0)))
    mod_all = _ada(c8, w_ada, b_ada)
    cos_t, sin_t = _rope_tables(s)
    e_mat, m_mat = _nsa_constants(s)
    fg = final_norm_g.reshape(1, d)
    for l in range(depth):
        mod = mod_all[l, :b].reshape(b, N_ADA, d)
        wmain, qg, kvg, wq, wkv, w1aug, pos2, w1f, w2p = _layer_weights(
            w_in[l], mla_q_norm_g[l], mla_kv_norm_g[l], mla_w_uq[l], mla_w_ukv[l], nsa_cmp_pos_k[l],
            nsa_cmp_pos_v[l], nsa_cmp_k_w1[l], nsa_cmp_k_w2[l], nsa_cmp_v_w1[l], nsa_cmp_v_w2[l])
        z = _in_proj(x, mod, norm_mix_g[l].reshape(1, d), wmain, qg, kvg, wq, wkv, cos_t, sin_t)
        o_a = _swa(z["qA"], z["kA"], z["vA"], swa_sinks[l])
        o_b = _mla(z["qB"], z["kB"], z["vB"])
        ch = z["kcvc"].reshape(b, s // NSA_CMP_STRIDE, NSA_CMP_STRIDE * LANES)
        kcmp, vcmp = _nsa_compress(ch, w1aug, pos2, w1f, w2p)
        o_c = _nsa(z["qC"], kcmp, vcmp, z["kv4"], z["gl"], e_mat, m_mat)
        o_d = _moba(z["qD"], z["kD"], z["vD"])
        wo4 = w_out[l].astype(BF16).reshape(4, N_HEADS * HEAD_DIM, d)
        x = _post(o_a, o_b, o_c, o_d, x, mod, wo4, norm_mlp_g[l].reshape(1, d), w_up[l].astype(BF16),
                  w_down[l].astype(BF16), fg, final=(l == depth - 1))
    return x
```

```python
import functools
import math

import numpy as np
import jax
import jax.numpy as jnp
from jax import lax
from jax.experimental import pallas as pl
from jax.experimental.pallas import tpu as pltpu

F32 = jnp.float32
BF16 = jnp.bfloat16

D_MODEL = 1024
DEPTH = 2
HEAD_DIM = 64
LANES = 128
QB = 128
NEG = -1e30
TINY = 1e-30
BIG = 1e9
RMS_EPS = 1e-6
N_ADA = 6
N_HEADS = 4

SWA_KV_HEADS = 2
SWA_WINDOW = 128
MLA_Q_RANK = 192
MLA_KV_RANK = 128
MLA_NOPE = 64
MLA_ROPE = 32
ROPE_THETA = 10000.0
NSA_CMP_LEN = 32
NSA_CMP_STRIDE = 16
NSA_CMP_HIDDEN = 128
NSA_SEL_BLOCK = 64
NSA_SEL_SHIFT = 6
NSA_TOPN = 16
NSA_WINDOW = 512
NSA_N_BRANCH = 3
MOBA_BLOCK = 256
MOBA_SHIFT = 8
MOBA_TOPK = 3
MOBA_LANE0 = 96
D_FF = 4 * D_MODEL
N_ALIBI = 3 * N_HEADS

_S_ALL = [2.0 ** (-8.0 * (i + 1) / N_ALIBI) for i in range(N_ALIBI)]
SLOPES_A = [float(np.float32(v)) for v in _S_ALL[0::3]]
SLOPES_C = [float(np.float32(v)) for v in _S_ALL[1::3]]
SLOPES_D = [float(np.float32(v)) for v in _S_ALL[2::3]]

VMEM_LIMIT = 56 * 1024 * 1024

_NT = (((1,), (1,)), ((), ()))


def _cparams(sem):
    return pltpu.CompilerParams(dimension_semantics=sem, vmem_limit_bytes=VMEM_LIMIT)


def _split3(a):
    hi = a.astype(BF16)
    r1 = a - hi.astype(F32)
    mid = r1.astype(BF16)
    lo = (r1 - mid.astype(F32)).astype(BF16)
    return hi, mid, lo


def _dot(a, b):
    return jnp.dot(a, b, preferred_element_type=F32)


def _dot_nt(a, b):
    return lax.dot_general(a, b, _NT, preferred_element_type=F32)


def _split3_const(x):
    x = np.float32(x)
    hi = np.float32(np.asarray(x, dtype=BF16))
    mid = np.float32(np.asarray(np.float32(x - hi), dtype=BF16))
    lo = np.float32(np.asarray(np.float32(x - hi - mid), dtype=BF16))
    return float(hi), float(mid), float(lo)


def _pieces_lanes(lane, pieces, repeat):
    out = jnp.zeros(lane.shape, F32)
    for i, pc in enumerate(pieces):
        lo = HEAD_DIM + i * repeat
        out = jnp.where(jnp.logical_and(lane >= lo, lane < lo + repeat), pc, out)
    return out


def _pos_lanes(lane, pos):
    hi = ((pos >> 7) << 7).astype(F32)
    lo = (pos & 127).astype(F32)
    k = lane - HEAD_DIM
    return jnp.where(jnp.logical_and(k >= 0, k < 6), jnp.where((k & 1) == 0, hi, lo), 0.0)


def _ada_kernel(c_ref, w_ref, b_ref, o_ref):
    c = c_ref[...]
    a = c * jax.nn.sigmoid(c)
    w = w_ref[0]
    a_hi = a.astype(BF16)
    a_lo = (a - a_hi.astype(F32)).astype(BF16)
    w_hi = w.astype(BF16)
    w_lo = (w - w_hi.astype(F32)).astype(BF16)
    o_ref[0] = _dot(a_hi, w_hi) + _dot(a_hi, w_lo) + _dot(a_lo, w_hi) + b_ref[0]


def _ada(c8, w_ada, b_ada):
    depth, d, n = w_ada.shape
    nb = n // d
    return pl.pallas_call(
        _ada_kernel,
        grid=(depth, nb),
        in_specs=[
            pl.BlockSpec((8, d), lambda l, j: (0, 0)),
            pl.BlockSpec((1, d, d), lambda l, j: (l, 0, j)),
            pl.BlockSpec((1, 1, d), lambda l, j: (l, 0, j)),
        ],
        out_specs=pl.BlockSpec((1, 8, d), lambda l, j: (l, 0, j)),
        out_shape=jax.ShapeDtypeStruct((depth, 8, n), F32),
        compiler_params=_cparams(("parallel", "parallel")),
        name="ada_mod",
    )(c8, w_ada, b_ada.reshape(depth, 1, n))


_IN_GROUPS = (("mla", 640), ("qA", 512), ("kA", 256), ("vA", 256), ("qC", 512), ("kcvc", 128),
              ("kvC", 640), ("gl", 128), ("qD", 512), ("kD", 512), ("vD", 512))
_IN_OFFS = {}
_o = 0
for _n, _w in _IN_GROUPS:
    _IN_OFFS[_n] = (_o, _o + _w)
    _o += _w
IN_COLS = _o


def _in_proj_kernel(x_ref, mod_ref, g_ref, w_ref, qg_ref, kvg_ref, wq_ref, wkv_ref, cos_ref, sin_ref,
                    qB_ref, kB_ref, vB_ref, qA_ref, kA_ref, vA_ref, qC_ref, kcvc_ref, kvC_ref, gl_ref,
                    qD_ref, kD_ref, vD_ref):
    x = x_ref[0]
    y = x * lax.rsqrt(jnp.mean(x * x, axis=-1, keepdims=True) + RMS_EPS) * g_ref[...]
    h = y * (1.0 + mod_ref[0, 1:2, :]) + mod_ref[0, 0:1, :]
    hb = h.astype(BF16)

    def proj(name):
        a, b = _IN_OFFS[name]
        return _dot(hb, w_ref[:, a:b])

    for name, ref in (("qA", qA_ref), ("kA", kA_ref), ("vA", vA_ref), ("kcvc", kcvc_ref), ("gl", gl_ref)):
        ref[0] = proj(name).astype(ref.dtype)

    ts = x.shape[0]
    lane = lax.broadcasted_iota(jnp.int32, (ts, LANES), 1)
    pos = lax.broadcasted_iota(jnp.int32, (ts, LANES), 0) + pl.program_id(1) * ts
    one64 = jnp.where(lane == HEAD_DIM, 1.0, 0.0)
    ones3 = jnp.where(jnp.logical_and(lane >= HEAD_DIM, lane < HEAD_DIM + 3), 1.0, 0.0)
    blk_hot = jnp.where(lane - MOBA_LANE0 == (pos >> MOBA_SHIFT), 1.0, 0.0)
    posf = pos.astype(F32)

    zq = proj("qC")
    for hd in range(N_HEADS):
        sl = slice(hd * LANES, (hd + 1) * LANES)
        qC_ref[0, :, sl] = (zq[:, sl] + _pieces_lanes(lane, _split3_const(SLOPES_C[hd]), repeat=2)).astype(BF16)
    pos_lanes = _pos_lanes(lane, pos)
    sel_hot = jnp.where(lane == (pos >> NSA_SEL_SHIFT), 1.0, 0.0)
    zkv = proj("kvC")
    kvC_ref[0, :, 0:LANES] = (zkv[:, 0:LANES] + pos_lanes).astype(BF16)
    kvC_ref[0, :, LANES:2 * LANES] = sel_hot.astype(BF16)
    kvC_ref[0, :, 2 * LANES:3 * LANES] = (zkv[:, 2 * LANES:3 * LANES] + one64).astype(BF16)
    kvC_ref[0, :, 3 * LANES:4 * LANES] = (zkv[:, 3 * LANES:4 * LANES] + pos_lanes).astype(BF16)
    kvC_ref[0, :, 4 * LANES:5 * LANES] = (zkv[:, 4 * LANES:5 * LANES] + one64).astype(BF16)
    zq, zk, zv = proj("qD"), proj("kD"), proj("vD")
    for hd in range(N_HEADS):
        sl = slice(hd * LANES, (hd + 1) * LANES)
        hi, mid, lo = _split3(posf * SLOPES_D[hd])
        alibi = jnp.where(lane == HEAD_DIM, hi.astype(F32),
                          jnp.where(lane == HEAD_DIM + 1, mid.astype(F32),
                                    jnp.where(lane == HEAD_DIM + 2, lo.astype(F32), 0.0)))
        qD_ref[0, :, sl] = (zq[:, sl] + ones3).astype(BF16)
        kD_ref[0, :, sl] = (zk[:, sl] + alibi + blk_hot).astype(BF16)
        vD_ref[0, :, sl] = (zv[:, sl] + one64).astype(BF16)

    zb = proj("mla")
    cq = zb[:, 0:256]
    ckv = zb[:, 256:384]
    kpm = zb[:, 384:512]
    kpr = zb[:, 512:640]
    cqn = cq * lax.rsqrt(jnp.sum(cq * cq, axis=-1, keepdims=True) * (1.0 / MLA_Q_RANK) + RMS_EPS) * qg_ref[...]
    ckvn = ckv * lax.rsqrt(jnp.mean(ckv * ckv, axis=-1, keepdims=True) + RMS_EPS) * kvg_ref[...]
    qall = _dot(cqn.astype(BF16), wq_ref[...])
    kvall = _dot(ckvn.astype(BF16), wkv_ref[...])
    cosq = cos_ref[...]
    sinq = sin_ref[...]
    kpe = kpm * cosq + kpr * sinq
    for hd in range(N_HEADS):
        lo = hd * 2 * LANES
        sl = slice(hd * LANES, (hd + 1) * LANES)
        qB_ref[0, :, sl] = (qall[:, lo:lo + LANES] * cosq + qall[:, lo + LANES:lo + 2 * LANES] * sinq).astype(BF16)
        kB_ref[0, :, sl] = (kvall[:, lo:lo + LANES] + kpe).astype(BF16)
        vB_ref[0, :, sl] = (kvall[:, lo + LANES:lo + 2 * LANES] + one64).astype(BF16)


def _in_proj(x, mod, g, wmain, qg, kvg, wq, wkv, cos_t, sin_t, ts=256):
    b, s, d = x.shape
    nt = s // ts
    tok = lambda w: pl.BlockSpec((1, ts, w), lambda i, j: (i, j, 0))
    full = lambda a: pl.BlockSpec(a.shape, lambda i, j: (0,) * a.ndim)
    outs = [("qB", 512, BF16), ("kB", 512, BF16), ("vB", 512, BF16), ("qA", 512, BF16), ("kA", 256, BF16),
            ("vA", 256, BF16), ("qC", 512, BF16), ("kcvc", 128, BF16), ("kvC", 640, BF16), ("gl", 128, F32),
            ("qD", 512, BF16), ("kD", 512, BF16), ("vD", 512, BF16)]
    res = pl.pallas_call(
        _in_proj_kernel,
        grid=(b, nt),
        in_specs=[tok(d), pl.BlockSpec((1, N_ADA, d), lambda i, j: (i, 0, 0)), full(g), full(wmain), full(qg),
                  full(kvg), full(wq), full(wkv),
                  pl.BlockSpec((ts, LANES), lambda i, j: (j, 0)), pl.BlockSpec((ts, LANES), lambda i, j: (j, 0))],
        out_specs=[tok(w) for _, w, _ in outs],
        out_shape=[jax.ShapeDtypeStruct((b, s, w), dt) for _, w, dt in outs],
        compiler_params=_cparams(("parallel", "parallel")),
        name="in_proj",
    )(x, mod, g, wmain, qg, kvg, wq, wkv, cos_t, sin_t)
    return {n: r for (n, _, _), r in zip(outs, res)}


def _flash_ones(q, k, v, m, acc, scale=None, bias=None):
    s = _dot_nt(q, k)
    if scale is not None:
        s = s * scale
    if bias is not None:
        s = s + bias
    m_new = jnp.maximum(m, jnp.max(s, axis=1, keepdims=True))
    p = jnp.exp(s - m_new)
    acc = jnp.exp(m - m_new) * acc + _dot(p.astype(BF16), v)
    return m_new, acc


def _finish_ones(acc):
    return acc[:, :HEAD_DIM] * (1.0 / jnp.maximum(acc[:, HEAD_DIM:HEAD_DIM + 1], TINY))


def _dist(tq, tk, t0, k0):
    r = lax.broadcasted_iota(jnp.int32, (tq, tk), 0)
    c = lax.broadcasted_iota(jnp.int32, (tq, tk), 1)
    return (r - c) + (t0 - k0)


def _pack_heads(os):
    return jnp.concatenate([o[:, :HEAD_DIM] for o in os], axis=1)


def _swa_kernel(sink_ref, q_ref, kp_ref, kc_ref, vp_ref, vc_ref, o_ref):
    n = pl.program_id(1)
    dist_c = _dist(QB, QB, 0, 0)
    dist_p = dist_c + QB
    ok_c = dist_c >= 0
    ok_p = dist_p < jnp.where(n > 0, SWA_WINDOW, 0)
    dc = dist_c.astype(F32)
    dp = dist_p.astype(F32)
    outs = []
    for hd in range(N_HEADS):
        kv = hd // (N_HEADS // SWA_KV_HEADS)
        q = q_ref[0, :, hd * LANES:(hd + 1) * LANES]
        ksl = slice(kv * LANES, (kv + 1) * LANES)
        slope = SLOPES_A[hd]
        sink = sink_ref[hd]
        s_c = jnp.where(ok_c, _dot_nt(q, kc_ref[0, :, ksl]) - slope * dc, NEG)
        s_p = jnp.where(ok_p, _dot_nt(q, kp_ref[0, :, ksl]) - slope * dp, NEG)
        m = jnp.maximum(jnp.maximum(jnp.max(s_c, axis=1, keepdims=True), jnp.max(s_p, axis=1, keepdims=True)), sink)
        e_c = jnp.where(ok_c, jnp.exp(s_c - m), 0.0)
        e_p = jnp.where(ok_p, jnp.exp(s_p - m), 0.0)
        den = jnp.sum(e_c, axis=1, keepdims=True) + jnp.sum(e_p, axis=1, keepdims=True) + jnp.exp(sink - m)
        inv = 1.0 / jnp.maximum(den, TINY)
        o = _dot((e_c * inv).astype(BF16), vc_ref[0, :, ksl]) + _dot((e_p * inv).astype(BF16), vp_ref[0, :, ksl])
        outs.append(o)
    o_ref[0] = _pack_heads(outs).astype(o_ref.dtype)


def _swa(qA, kA, vA, sinks):
    b, s, _ = qA.shape
    nq = s // QB
    cur = lambda w: pl.BlockSpec((1, QB, w), lambda i, j: (i, j, 0))
    prev = lambda w: pl.BlockSpec((1, QB, w), lambda i, j: (i, jnp.maximum(j - 1, 0), 0))
    return pl.pallas_call(
        _swa_kernel,
        grid=(b, nq),
        in_specs=[pl.BlockSpec(memory_space=pltpu.SMEM), cur(512), prev(256), cur(256), prev(256), cur(256)],
        out_specs=cur(N_HEADS * HEAD_DIM),
        out_shape=jax.ShapeDtypeStruct((b, s, N_HEADS * HEAD_DIM), BF16),
        compiler_params=_cparams(("parallel", "parallel")),
        name="swa",
    )(sinks, qA, kA, kA, vA, vA)


MLA_SCALE = float((MLA_NOPE + MLA_ROPE) ** -0.5)


def _mla_kernel(q_ref, k_ref, v_ref, o_ref, *, tq, tk):
    t0 = pl.program_id(2) * tq
    qs = [q_ref[0, :, hd * LANES:(hd + 1) * LANES] for hd in range(2)]

    def step(k0, carry, causal):
        k0 = pl.multiple_of(k0, tk)
        bias = jnp.where(_dist(tq, tk, t0, k0) >= 0, 0.0, NEG) if causal else None
        out = []
        for hd in range(2):
            sl = slice(hd * LANES, (hd + 1) * LANES)
            m, acc = carry[hd]
            out.append(_flash_ones(qs[hd], k_ref[0, pl.ds(k0, tk), sl], v_ref[0, pl.ds(k0, tk), sl], m, acc,
                                   scale=MLA_SCALE, bias=bias))
        return tuple(out)

    init = tuple((jnp.full((tq, 1), NEG, F32), jnp.zeros((tq, LANES), F32)) for _ in range(2))
    n_past = t0 // tk
    carry = step(n_past * tk, init, True)
    carry = lax.fori_loop(0, n_past, lambda c, cr: step(c * tk, cr, False), carry)
    o_ref[0] = jnp.concatenate([_finish_ones(acc) for (_, acc) in carry], axis=1).astype(o_ref.dtype)


def _mla(qB, kB, vB, tq=256, tk=512):
    b, s, _ = qB.shape
    tk = min(tk, s)
    nq = s // tq
    return pl.pallas_call(
        functools.partial(_mla_kernel, tq=tq, tk=tk),
        grid=(b, 2, nq),
        in_specs=[pl.BlockSpec((1, tq, 2 * LANES), lambda bi, p, i: (bi, i, p)),
                  pl.BlockSpec((1, s, 2 * LANES), lambda bi, p, i: (bi, 0, p)),
                  pl.BlockSpec((1, s, 2 * LANES), lambda bi, p, i: (bi, 0, p))],
        out_specs=pl.BlockSpec((1, tq, LANES), lambda bi, p, i: (bi, i, p)),
        out_shape=jax.ShapeDtypeStruct((b, s, N_HEADS * HEAD_DIM), BF16),
        compiler_params=_cparams(("parallel", "parallel", "arbitrary")),
        name="mla_attn",
    )(qB, kB, vB)


def _gelu_tanh(x):
    return x * (0.5 * (1.0 + jnp.tanh(math.sqrt(2.0 / math.pi) * (x + 0.044715 * (x * x * x)))))


def _nsa_cmp_kernel(ch_ref, w1_ref, pos_ref, w1f_ref, w2_ref, kc_ref, vc_ref):
    nc = ch_ref.shape[1]
    u = _dot(ch_ref[0], w1_ref[...])
    for idx, out in ((0, kc_ref), (1, vc_ref)):
        pos = jnp.broadcast_to(pos_ref[idx], (8, pos_ref.shape[2]))
        w1f = w1f_ref[idx]
        p_hi = pos.astype(BF16)
        p_lo = (pos - p_hi.astype(F32)).astype(BF16)
        w_hi = w1f.astype(BF16)
        w_lo = (w1f - w_hi.astype(F32)).astype(BF16)
        posb = (_dot(p_hi, w_hi) + _dot(p_hi, w_lo) + _dot(p_lo, w_hi))[0:1, :]
        top = u[:, (2 * idx) * LANES:(2 * idx + 1) * LANES]
        bot = u[:, (2 * idx + 1) * LANES:(2 * idx + 2) * LANES]
        hid = top + pltpu.roll(bot, nc - 1, 0) + posb
        res = _dot(_gelu_tanh(hid).astype(BF16), w2_ref[idx])
        if idx == 0:
            lane = lax.broadcasted_iota(jnp.int32, (nc, LANES), 1)
            cend = lax.broadcasted_iota(jnp.int32, (nc, LANES), 0) * NSA_CMP_STRIDE + (NSA_CMP_LEN - 1)
            res = res + _pos_lanes(lane, cend)
        out[0] = res.astype(out.dtype)


def _nsa_compress(ch, w1aug, pos2, w1f, w2p):
    b, nc, w = ch.shape
    full = lambda a: pl.BlockSpec(a.shape, lambda i: (0,) * a.ndim)
    return pl.pallas_call(
        _nsa_cmp_kernel,
        grid=(b,),
        in_specs=[pl.BlockSpec((1, nc, w), lambda i: (i, 0, 0)), full(w1aug), full(pos2), full(w1f), full(w2p)],
        out_specs=[pl.BlockSpec((1, nc, LANES), lambda i: (i, 0, 0))] * 2,
        out_shape=[jax.ShapeDtypeStruct((b, nc, LANES), BF16)] * 2,
        compiler_params=_cparams(("parallel",)),
        name="nsa_compress",
    )(ch, w1aug, pos2, w1f, w2p)


NSA_CHUNK = 256


def _dist_stacked(rows, tk, t0, k0):
    r = lax.broadcasted_iota(jnp.int32, (rows, tk), 0) & (QB - 1)
    c = lax.broadcasted_iota(jnp.int32, (rows, tk), 1)
    return (r - c) + (t0 - k0)


def _nsa_kernel(q_ref, kc_ref, vc_ref, kv_ref, gl_ref, mm_ref, o_ref, imp_sc):
    n = pl.program_id(1)
    t0 = n * QB
    nc = kc_ref.shape[1]
    nbl = LANES
    rows = N_HEADS * QB
    qh = [q_ref[0, :, hd * LANES:(hd + 1) * LANES] for hd in range(N_HEADS)]
    qs = jnp.concatenate(qh, axis=0)

    r = (lax.broadcasted_iota(jnp.int32, (rows, nc), 0) & (QB - 1)) + t0
    cidx = lax.broadcasted_iota(jnp.int32, (rows, nc), 1)
    ok = jnp.logical_and(r >= cidx * NSA_CMP_STRIDE + (NSA_CMP_LEN - 1), cidx < nc - 1)
    s = jnp.where(ok, _dot_nt(qs, kc_ref[0]), NEG)
    m = jnp.max(s, axis=1, keepdims=True)
    e = jnp.where(ok, jnp.exp(s - m), 0.0)
    p = e * (1.0 / jnp.maximum(jnp.sum(e, axis=1, keepdims=True), TINY))
    o_cmp = _dot(p.astype(BF16), vc_ref[0])
    psum = p[0:QB]
    for hd in range(1, N_HEADS):
        psum = psum + p[hd * QB:(hd + 1) * QB]

    mm = mm_ref[...]
    imp = None
    for piece in _split3(psum):
        d = _dot(piece, mm)
        imp = d if imp is None else imp + d
    imp_t = imp.T
    jrow = lax.broadcasted_iota(jnp.int32, (nbl, QB), 0)
    tcol = lax.broadcasted_iota(jnp.int32, (nbl, QB), 1) + t0
    cur = tcol >> 6
    cand = jrow <= cur
    forced = jnp.logical_or(jnp.logical_or(jrow == 0, jrow == cur), jrow == cur - 1)
    a = jnp.where(cand, jnp.where(forced, BIG, imp_t), NEG)
    imp_sc[...] = a
    n_cand = (t0 + QB - 1) // NSA_SEL_BLOCK + 1

    def rank_body(i, cnt):
        row = imp_sc[pl.ds(i, 1), :]
        c_ge = jnp.where(row >= a, 1.0, 0.0)
        c_gt = jnp.where(row > a, 1.0, 0.0)
        return cnt + jnp.where(jrow > i, c_ge, c_gt)

    rank = lax.fori_loop(0, n_cand, rank_body, jnp.zeros((nbl, QB), F32))
    sel_t = jnp.where(jnp.logical_and(rank < NSA_TOPN, cand), 0.0, NEG)
    selneg = sel_t.T.astype(BF16)

    q_sel = jnp.concatenate([jnp.concatenate([qh[hd], selneg], axis=1) for hd in range(N_HEADS)], axis=0)
    own = t0 // NSA_CHUNK

    def sel_step(c, carry, causal):
        k0 = pl.multiple_of(c * NSA_CHUNK, NSA_CHUNK)
        bias = jnp.where(_dist_stacked(rows, NSA_CHUNK, t0, k0) >= 0, 0.0, NEG) if causal else None
        return _flash_ones(q_sel, kv_ref[0, pl.ds(k0, NSA_CHUNK), 0:2 * LANES],
                           kv_ref[0, pl.ds(k0, NSA_CHUNK), 2 * LANES:3 * LANES], carry[0], carry[1], bias=bias)

    init = (jnp.full((rows, 1), NEG, F32), jnp.zeros((rows, LANES), F32))
    carry = sel_step(own, init, True)
    carry = lax.fori_loop(0, own, lambda c, cr: sel_step(c, cr, False), carry)
    o_sel = _finish_ones(carry[1])

    slab = NSA_WINDOW + QB
    k0w = pl.multiple_of(jnp.maximum(n - NSA_WINDOW // QB, 0) * QB, QB)
    dist = _dist_stacked(rows, slab, t0, k0w)
    okw = jnp.logical_and(dist >= 0, dist < NSA_WINDOW)
    s = jnp.where(okw, _dot_nt(qs, kv_ref[0, pl.ds(k0w, slab), 3 * LANES:4 * LANES]), NEG)
    pw = jnp.exp(s - jnp.max(s, axis=1, keepdims=True))
    o_win = _finish_ones(_dot(pw.astype(BF16), kv_ref[0, pl.ds(k0w, slab), 4 * LANES:5 * LANES]))

    g = jax.nn.sigmoid(gl_ref[0])
    outs = []
    for hd in range(N_HEADS):
        rs = slice(hd * QB, (hd + 1) * QB)
        outs.append(g[:, hd:hd + 1] * o_cmp[rs, :HEAD_DIM] + g[:, N_HEADS + hd:N_HEADS + hd + 1] * o_sel[rs]
                    + g[:, 2 * N_HEADS + hd:2 * N_HEADS + hd + 1] * o_win[rs])
    o_ref[0] = jnp.concatenate(outs, axis=1).astype(o_ref.dtype)


def _nsa(qC, kcmp, vcmp, kvC, gl, m_mat):
    b, s, _ = qC.shape
    nq = s // QB
    nc = kcmp.shape[1]
    assert s >= NSA_WINDOW + QB and s % NSA_CHUNK == 0
    return pl.pallas_call(
        _nsa_kernel,
        grid=(b, nq),
        in_specs=[pl.BlockSpec((1, QB, N_HEADS * LANES), lambda i, j: (i, j, 0)),
                  pl.BlockSpec((1, nc, LANES), lambda i, j: (i, 0, 0)),
                  pl.BlockSpec((1, nc, LANES), lambda i, j: (i, 0, 0)),
                  pl.BlockSpec((1, s, 5 * LANES), lambda i, j: (i, 0, 0)),
                  pl.BlockSpec((1, QB, LANES), lambda i, j: (i, j, 0)),
                  pl.BlockSpec(m_mat.shape, lambda i, j: (0, 0))],
        out_specs=pl.BlockSpec((1, QB, N_HEADS * HEAD_DIM), lambda i, j: (i, j, 0)),
        out_shape=jax.ShapeDtypeStruct((b, s, N_HEADS * HEAD_DIM), BF16),
        scratch_shapes=[pltpu.VMEM((LANES, QB), F32)],
        compiler_params=_cparams(("parallel", "arbitrary")),
        name="nsa_attn",
    )(qC, kcmp, vcmp, kvC, gl, m_mat)


MOBA_NBR = LANES - MOBA_LANE0


def _moba_kernel(q_ref, k_ref, v_ref, o_ref, kmean_sc, gate_sc):
    tq = MOBA_BLOCK
    cur = pl.program_id(2)
    t0 = cur * tq
    n_blk = k_ref.shape[1] // MOBA_BLOCK
    lane = lax.broadcasted_iota(jnp.int32, (1, LANES), 1)

    @pl.when(cur == 0)
    def _():
        kmean_sc[...] = jnp.zeros(kmean_sc.shape, F32)
        for hd in range(2):
            for jb in range(n_blk):
                blk = k_ref[0, jb * MOBA_BLOCK:(jb + 1) * MOBA_BLOCK, hd * LANES:(hd + 1) * LANES].astype(F32)
                kmean_sc[hd, jb:jb + 1, :] = jnp.where(lane < HEAD_DIM, jnp.mean(blk, axis=0, keepdims=True), 0.0)

    jrow = lax.broadcasted_iota(jnp.int32, (MOBA_NBR, tq), 0)
    qs = [q_ref[0, :, hd * LANES:(hd + 1) * LANES] for hd in range(2)]
    gates = []
    for hd in range(2):
        gate_t = _dot_nt(kmean_sc[hd].astype(BF16), qs[hd])[0:MOBA_NBR, :]
        a = jnp.where(jrow < cur, gate_t, NEG)
        gate_sc[hd] = a
        gates.append(a)

    def rank_body(i, cnts):
        out = []
        for hd in range(2):
            row = gate_sc[hd, pl.ds(i, 1), :]
            c_ge = jnp.where(row >= gates[hd], 1.0, 0.0)
            c_gt = jnp.where(row > gates[hd], 1.0, 0.0)
            out.append(cnts[hd] + jnp.where(jrow > i, c_ge, c_gt))
        return tuple(out)

    ranks = lax.fori_loop(0, cur, rank_body, tuple(jnp.zeros((MOBA_NBR, tq), F32) for _ in range(2)))
    q_sel = []
    for hd in range(2):
        sel_t = jnp.where(jnp.logical_and(ranks[hd] < MOBA_TOPK, jrow < cur), 0.0, NEG)
        full_t = jnp.concatenate([jnp.zeros((MOBA_LANE0, tq), F32), sel_t], axis=0)
        q_sel.append((qs[hd].astype(F32) + full_t.T).astype(BF16))

    def step(k0, carry, width, own_blk):
        k0 = pl.multiple_of(k0, MOBA_BLOCK)
        bias = jnp.where(_dist(tq, width, t0, k0) >= 0, 0.0, NEG) if own_blk else None
        out = []
        for hd in range(2):
            sl = slice(hd * LANES, (hd + 1) * LANES)
            m, acc = carry[hd]
            q = qs[hd] if own_blk else q_sel[hd]
            out.append(_flash_ones(q, k_ref[0, pl.ds(k0, width), sl], v_ref[0, pl.ds(k0, width), sl], m, acc,
                                   bias=bias))
        return tuple(out)

    init = tuple((jnp.full((tq, 1), NEG, F32), jnp.zeros((tq, LANES), F32)) for _ in range(2))
    carry = step(t0, init, MOBA_BLOCK, True)
    carry = lax.fori_loop(0, (cur + 1) // 2, lambda c, cr: step(c * (2 * MOBA_BLOCK), cr, 2 * MOBA_BLOCK, False),
                          carry)
    o_ref[0] = jnp.concatenate([_finish_ones(acc) for (_, acc) in carry], axis=1).astype(o_ref.dtype)


def _moba(qD, kD, vD):
    b, s, _ = qD.shape
    tq = MOBA_BLOCK
    assert s % (2 * MOBA_BLOCK) == 0 and s // MOBA_BLOCK <= MOBA_NBR
    return pl.pallas_call(
        _moba_kernel,
        grid=(b, 2, s // tq),
        in_specs=[pl.BlockSpec((1, tq, 2 * LANES), lambda i, p, j: (i, j, p)),
                  pl.BlockSpec((1, s, 2 * LANES), lambda i, p, j: (i, 0, p)),
                  pl.BlockSpec((1, s, 2 * LANES), lambda i, p, j: (i, 0, p))],
        out_specs=pl.BlockSpec((1, tq, LANES), lambda i, p, j: (i, j, p)),
        out_shape=jax.ShapeDtypeStruct((b, s, N_HEADS * HEAD_DIM), BF16),
        scratch_shapes=[pltpu.VMEM((2, LANES, LANES), F32), pltpu.VMEM((2, MOBA_NBR, tq), F32)],
        compiler_params=_cparams(("parallel", "parallel", "arbitrary")),
        name="moba_attn",
    )(qD, kD, vD)


def _post_kernel(oa_ref, ob_ref, oc_ref, od_ref, x_ref, mod_ref, wo_ref, g_ref, wu_ref, wd_ref, fg_ref, out_ref,
                 *, final, ffc):
    x = x_ref[0]
    mix = None
    for idx, ref in enumerate((oa_ref, ob_ref, oc_ref, od_ref)):
        d = _dot(ref[0], wo_ref[idx])
        mix = d if mix is None else mix + d
    x1 = x + mod_ref[0, 2:3, :] * mix
    y = x1 * lax.rsqrt(jnp.mean(x1 * x1, axis=-1, keepdims=True) + RMS_EPS) * g_ref[...]
    hb = (y * (1.0 + mod_ref[0, 4:5, :]) + mod_ref[0, 3:4, :]).astype(BF16)
    acc = None
    for c in range(wu_ref.shape[1] // ffc):
        hid = jnp.maximum(_dot(hb, wu_ref[:, c * ffc:(c + 1) * ffc]), 0.0)
        d = _dot((hid * hid).astype(BF16), wd_ref[c * ffc:(c + 1) * ffc, :])
        acc = d if acc is None else acc + d
    x2 = x1 + mod_ref[0, 5:6, :] * acc
    if final:
        x2 = x2 * lax.rsqrt(jnp.mean(x2 * x2, axis=-1, keepdims=True) + RMS_EPS) * fg_ref[...]
    out_ref[0] = x2


def _post(oa, ob, oc, od, x, mod, wo4, g, wu, wd, fg, final, ts=256, ffc=1024):
    b, s, d = x.shape
    nt = s // ts
    tok = lambda w: pl.BlockSpec((1, ts, w), lambda i, j: (i, j, 0))
    full = lambda a: pl.BlockSpec(a.shape, lambda i, j: (0,) * a.ndim, pipeline_mode=pl.Buffered(1))
    hw = N_HEADS * HEAD_DIM
    return pl.pallas_call(
        functools.partial(_post_kernel, final=final, ffc=ffc),
        grid=(b, nt),
        in_specs=[tok(hw), tok(hw), tok(hw), tok(hw), tok(d), pl.BlockSpec((1, N_ADA, d), lambda i, j: (i, 0, 0)),
                  full(wo4), full(g), full(wu), full(wd), full(fg)],
        out_specs=tok(d),
        out_shape=jax.ShapeDtypeStruct((b, s, d), F32),
        compiler_params=_cparams(("parallel", "parallel")),
        name="post_mlp",
    )(oa, ob, oc, od, x, mod, wo4, g, wu, wd, fg)


def _pad_heads(w, n_heads, scale=None):
    k = w.shape[0]
    w = w.reshape(k, n_heads, HEAD_DIM)
    if scale is not None:
        w = w * scale
    return jnp.pad(w, ((0, 0), (0, 0), (0, LANES - HEAD_DIM))).reshape(k, n_heads * LANES)


def _rot_half_cols(w):
    half = w.shape[1] // 2
    return jnp.concatenate([-w[:, half:], w[:, :half]], axis=1)


def _layer_weights(w_in, mla_qg, mla_kvg, w_uq, w_ukv, pos_k, pos_v, ck_w1, ck_w2, cv_w1, cv_w2):
    d = w_in.shape[0]
    z = lambda n: jnp.zeros((d, n), F32)
    o = 0

    def take(n):
        nonlocal o
        r = w_in[:, o:o + n]
        o += n
        return r

    qa, ka, va = take(256), take(128), take(128)
    cq, ckv, kpe = take(MLA_Q_RANK), take(MLA_KV_RANK), take(MLA_ROPE)
    qc, kc, vc, ks, vs, kw, vw, gl = take(256), take(64), take(64), take(64), take(64), take(64), take(64), take(12)
    qd, kd, vd = take(256), take(256), take(256)
    sc = 1.0 / 8.0
    kpe_main = jnp.concatenate([z(64), kpe, z(32)], axis=1)
    kpe_rot = jnp.concatenate([z(64), _rot_half_cols(kpe), z(32)], axis=1)
    p64 = lambda w: jnp.pad(w, ((0, 0), (0, LANES - w.shape[1])))
    groups = {
        "mla": jnp.concatenate([cq, z(64), ckv, kpe_main, kpe_rot], axis=1),
        "qA": _pad_heads(qa, 4, sc), "kA": _pad_heads(ka, 2), "vA": _pad_heads(va, 2),
        "qC": _pad_heads(qc, 4, sc), "kcvc": jnp.concatenate([kc, vc], axis=1),
        "kvC": jnp.concatenate([p64(ks), z(LANES), p64(vs), p64(kw), p64(vw)], axis=1), "gl": p64(gl),
        "qD": _pad_heads(qd, 4, sc), "kD": _pad_heads(kd, 4), "vD": _pad_heads(vd, 4),
    }
    wmain = jnp.concatenate([groups[n] for n, _ in _IN_GROUPS], axis=1).astype(BF16)

    dq = MLA_NOPE + MLA_ROPE
    wq_cols, wkv_cols = [], []
    for hd in range(N_HEADS):
        wh = w_uq[:, hd * dq:(hd + 1) * dq]
        zq = lambda n: jnp.zeros((MLA_Q_RANK, n), F32)
        wq_cols += [wh, zq(32), zq(64), _rot_half_cols(wh[:, MLA_NOPE:]), zq(32)]
        wk = w_ukv[:, hd * 128:(hd + 1) * 128]
        zk = jnp.zeros((MLA_KV_RANK, 64), F32)
        wkv_cols += [wk[:, :64], zk, wk[:, 64:], zk]
    wq = jnp.pad(jnp.concatenate(wq_cols, axis=1), ((0, 256 - MLA_Q_RANK), (0, 0))).astype(BF16)
    wkv = jnp.concatenate(wkv_cols, axis=1).astype(BF16)
    qg = jnp.pad(mla_qg, (0, 256 - MLA_Q_RANK)).reshape(1, 256)
    kvg = mla_kvg.reshape(1, MLA_KV_RANK)


    def aug(w1, is_v):
        w = w1.reshape(NSA_CMP_LEN, HEAD_DIM, NSA_CMP_HIDDEN)
        zz = jnp.zeros_like(w)
        w = jnp.concatenate([zz, w] if is_v else [w, zz], axis=1)
        w = w.reshape(NSA_CMP_LEN * LANES, NSA_CMP_HIDDEN)
        return w[:NSA_CMP_STRIDE * LANES], w[NSA_CMP_STRIDE * LANES:]

    kt, kb = aug(ck_w1, False)
    vt, vb = aug(cv_w1, True)
    w1aug = jnp.concatenate([kt, kb, vt, vb], axis=1).astype(BF16)
    pos2 = jnp.stack([pos_k.reshape(1, -1), pos_v.reshape(1, -1)])
    w1f = jnp.stack([ck_w1, cv_w1])
    w2p = jnp.stack([p64(ck_w2), p64(cv_w2)]).astype(BF16)
    return wmain, qg, kvg, wq, wkv, w1aug, pos2, w1f, w2p


def _rope_tables(s):
    half = MLA_ROPE // 2
    freqs = ROPE_THETA ** (-jnp.arange(half, dtype=F32) / half)
    ang = jnp.arange(s, dtype=F32)[:, None] * freqs[None, :]
    cos, sin = jnp.cos(ang), jnp.sin(ang)
    ones, zeros = jnp.ones((s, MLA_NOPE), F32), jnp.zeros((s, MLA_NOPE), F32)
    tail = jnp.zeros((s, LANES - MLA_NOPE - MLA_ROPE), F32)
    return (jnp.concatenate([ones, cos, cos, tail], axis=1), jnp.concatenate([zeros, sin, sin, tail], axis=1))


def _nsa_imp_matrix(s):
    n_sel = s // NSA_SEL_BLOCK
    nc = s // NSA_CMP_STRIDE
    assert n_sel <= LANES
    c = np.arange(nc)[:, None]
    j = np.arange(LANES)[None, :]
    m = ((c >= 4 * j - 1) & (c <= 4 * j + 3) & (c < nc - 1) & (j < n_sel)).astype(np.float32)
    return jnp.asarray(m, BF16)


def kernel(x, c, norm_mix_g, norm_mlp_g, w_ada, b_ada, w_in, w_out, swa_sinks, mla_q_norm_g, mla_kv_norm_g,
           mla_w_uq, mla_w_ukv, nsa_cmp_pos_k, nsa_cmp_pos_v, nsa_cmp_k_w1, nsa_cmp_k_w2, nsa_cmp_v_w1,
           nsa_cmp_v_w2, w_up, w_down, final_norm_g):
    b, s, d = x.shape
    depth = w_in.shape[0]
    c8 = jnp.pad(c, ((0, 8 - b), (0, 0)))
    mod_all = _ada(c8, w_ada, b_ada)
    cos_t, sin_t = _rope_tables(s)
    m_mat = _nsa_imp_matrix(s)
    fg = final_norm_g.reshape(1, d)
    for l in range(depth):
        mod = mod_all[l, :b].reshape(b, N_ADA, d)
        wmain, qg, kvg, wq, wkv, w1aug, pos2, w1f, w2p = _layer_weights(
            w_in[l], mla_q_norm_g[l], mla_kv_norm_g[l], mla_w_uq[l], mla_w_ukv[l], nsa_cmp_pos_k[l],
            nsa_cmp_pos_v[l], nsa_cmp_k_w1[l], nsa_cmp_k_w2[l], nsa_cmp_v_w1[l], nsa_cmp_v_w2[l])
        z = _in_proj(x, mod, norm_mix_g[l].reshape(1, d), wmain, qg, kvg, wq, wkv, cos_t, sin_t)
        o_a = _swa(z["qA"], z["kA"], z["vA"], swa_sinks[l])
        o_b = _mla(z["qB"], z["kB"], z["vB"])
        ch = z["kcvc"].reshape(b, s // NSA_CMP_STRIDE, NSA_CMP_STRIDE * LANES)
        kcmp, vcmp = _nsa_compress(ch, w1aug, pos2, w1f, w2p)
        o_c = _nsa(z["qC"], kcmp, vcmp, z["kvC"], z["gl"], m_mat)
        o_d = _moba(z["qD"], z["kD"], z["vD"])
        wo4 = w_out[l].astype(BF16).reshape(4, N_HEADS * HEAD_DIM, d)
        x = _post(o_a, o_b, o_c, o_d, x, mod, wo4, norm_mlp_g[l].reshape(1, d), w_up[l].astype(BF16),
                  w_down[l].astype(BF16), fg, final=(l == depth - 1))
    return x
```

```python
import functools
import math

import numpy as np
import jax
import jax.numpy as jnp
from jax import lax
from jax.experimental import pallas as pl
from jax.experimental.pallas import tpu as pltpu

F32 = jnp.float32
BF16 = jnp.bfloat16

D_MODEL = 1024
DEPTH = 2
HEAD_DIM = 64
LANES = 128
QB = 128
NEG = -1e30
TINY = 1e-30
BIG = 1e9
RMS_EPS = 1e-6
N_ADA = 6
N_HEADS = 4

SWA_KV_HEADS = 2
SWA_WINDOW = 128
MLA_Q_RANK = 192
MLA_KV_RANK = 128
MLA_NOPE = 64
MLA_ROPE = 32
ROPE_THETA = 10000.0
NSA_CMP_LEN = 32
NSA_CMP_STRIDE = 16
NSA_CMP_HIDDEN = 128
NSA_SEL_BLOCK = 64
NSA_SEL_SHIFT = 6
NSA_TOPN = 16
NSA_WINDOW = 512
NSA_N_BRANCH = 3
MOBA_BLOCK = 256
MOBA_SHIFT = 8
MOBA_TOPK = 3
MOBA_LANE0 = 96
D_FF = 4 * D_MODEL
N_ALIBI = 3 * N_HEADS

_S_ALL = [2.0 ** (-8.0 * (i + 1) / N_ALIBI) for i in range(N_ALIBI)]
SLOPES_A = [float(np.float32(v)) for v in _S_ALL[0::3]]
SLOPES_C = [float(np.float32(v)) for v in _S_ALL[1::3]]
SLOPES_D = [float(np.float32(v)) for v in _S_ALL[2::3]]

VMEM_LIMIT = 56 * 1024 * 1024

_NT = (((1,), (1,)), ((), ()))


def _cparams(sem):
    return pltpu.CompilerParams(dimension_semantics=sem, vmem_limit_bytes=VMEM_LIMIT)


def _split3(a):
    hi = a.astype(BF16)
    r1 = a - hi.astype(F32)
    mid = r1.astype(BF16)
    lo = (r1 - mid.astype(F32)).astype(BF16)
    return hi, mid, lo


def _dot(a, b):
    return jnp.dot(a, b, preferred_element_type=F32)


def _dot_nt(a, b):
    return lax.dot_general(a, b, _NT, preferred_element_type=F32)


def _split3_const(x):
    x = np.float32(x)
    hi = np.float32(np.asarray(x, dtype=BF16))
    mid = np.float32(np.asarray(np.float32(x - hi), dtype=BF16))
    lo = np.float32(np.asarray(np.float32(x - hi - mid), dtype=BF16))
    return float(hi), float(mid), float(lo)


def _pieces_lanes(lane, pieces, repeat):
    out = jnp.zeros(lane.shape, F32)
    for i, pc in enumerate(pieces):
        lo = HEAD_DIM + i * repeat
        out = jnp.where(jnp.logical_and(lane >= lo, lane < lo + repeat), pc, out)
    return out


def _pos_lanes(lane, pos):
    hi = ((pos >> 7) << 7).astype(F32)
    lo = (pos & 127).astype(F32)
    k = lane - HEAD_DIM
    return jnp.where(jnp.logical_and(k >= 0, k < 6), jnp.where((k & 1) == 0, hi, lo), 0.0)


def _ada_kernel(c_ref, w_ref, b_ref, o_ref):
    c = c_ref[...]
    a = c * jax.nn.sigmoid(c)
    w = w_ref[0]
    a_hi = a.astype(BF16)
    a_lo = (a - a_hi.astype(F32)).astype(BF16)
    w_hi = w.astype(BF16)
    w_lo = (w - w_hi.astype(F32)).astype(BF16)
    o_ref[0] = _dot(a_hi, w_hi) + _dot(a_hi, w_lo) + _dot(a_lo, w_hi) + b_ref[0]


def _ada(c8, w_ada, b_ada):
    depth, d, n = w_ada.shape
    nb = n // d
    return pl.pallas_call(
        _ada_kernel,
        grid=(depth, nb),
        in_specs=[
            pl.BlockSpec((8, d), lambda l, j: (0, 0)),
            pl.BlockSpec((1, d, d), lambda l, j: (l, 0, j)),
            pl.BlockSpec((1, 1, d), lambda l, j: (l, 0, j)),
        ],
        out_specs=pl.BlockSpec((1, 8, d), lambda l, j: (l, 0, j)),
        out_shape=jax.ShapeDtypeStruct((depth, 8, n), F32),
        compiler_params=_cparams(("parallel", "parallel")),
        name="ada_mod",
    )(c8, w_ada, b_ada.reshape(depth, 1, n))


_IN_GROUPS = (("mla", 640), ("qA", 512), ("kA", 256), ("vA", 256), ("qC", 512), ("kcvc", 128),
              ("kvC", 640), ("gl", 128), ("qD", 512), ("kD", 512), ("vD", 512))
_IN_OFFS = {}
_o = 0
for _n, _w in _IN_GROUPS:
    _IN_OFFS[_n] = (_o, _o + _w)
    _o += _w
IN_COLS = _o


def _in_proj_kernel(x_ref, mod_ref, g_ref, w_ref, qg_ref, kvg_ref, wq_ref, wkv_ref, cos_ref, sin_ref,
                    qB_ref, kB_ref, vB_ref, qA_ref, kA_ref, vA_ref, qC_ref, kcvc_ref, kvC_ref, gl_ref,
                    qD_ref, kD_ref, vD_ref):
    x = x_ref[0]
    y = x * lax.rsqrt(jnp.mean(x * x, axis=-1, keepdims=True) + RMS_EPS) * g_ref[...]
    h = y * (1.0 + mod_ref[0, 1:2, :]) + mod_ref[0, 0:1, :]
    hb = h.astype(BF16)

    def proj(name):
        a, b = _IN_OFFS[name]
        return _dot(hb, w_ref[:, a:b])

    for name, ref in (("kcvc", kcvc_ref), ("gl", gl_ref)):
        ref[0] = proj(name).astype(ref.dtype)

    ts = x.shape[0]
    lane = lax.broadcasted_iota(jnp.int32, (ts, LANES), 1)
    pos = lax.broadcasted_iota(jnp.int32, (ts, LANES), 0) + pl.program_id(1) * ts
    one64 = jnp.where(lane == HEAD_DIM, 1.0, 0.0)
    ones3 = jnp.where(jnp.logical_and(lane >= HEAD_DIM, lane < HEAD_DIM + 3), 1.0, 0.0)
    blk_hot = jnp.where(lane - MOBA_LANE0 == (pos >> MOBA_SHIFT), 1.0, 0.0)
    posf = pos.astype(F32)

    zq = proj("qC")
    for hd in range(N_HEADS):
        sl = slice(hd * LANES, (hd + 1) * LANES)
        qC_ref[0, :, sl] = (zq[:, sl] + _pieces_lanes(lane, _split3_const(SLOPES_C[hd]), repeat=2)).astype(BF16)
    pos_lanes = _pos_lanes(lane, pos)

    zq, zk, zv = proj("qA"), proj("kA"), proj("vA")
    for hd in range(N_HEADS):
        sl = slice(hd * LANES, (hd + 1) * LANES)
        qA_ref[0, :, sl] = (zq[:, sl] + _pieces_lanes(lane, _split3_const(SLOPES_A[hd]), repeat=2)).astype(BF16)
    for kv in range(SWA_KV_HEADS):
        sl = slice(kv * LANES, (kv + 1) * LANES)
        kA_ref[0, :, sl] = (zk[:, sl] + pos_lanes).astype(BF16)
        vA_ref[0, :, sl] = (zv[:, sl] + one64).astype(BF16)

    sel_hot = jnp.where(lane == (pos >> NSA_SEL_SHIFT), 1.0, 0.0)
    zkv = proj("kvC")
    kvC_ref[0, :, 0:LANES] = (zkv[:, 0:LANES] + pos_lanes).astype(BF16)
    kvC_ref[0, :, LANES:2 * LANES] = sel_hot.astype(BF16)
    kvC_ref[0, :, 2 * LANES:3 * LANES] = (zkv[:, 2 * LANES:3 * LANES] + one64).astype(BF16)
    kvC_ref[0, :, 3 * LANES:4 * LANES] = (zkv[:, 3 * LANES:4 * LANES] + pos_lanes).astype(BF16)
    kvC_ref[0, :, 4 * LANES:5 * LANES] = (zkv[:, 4 * LANES:5 * LANES] + one64).astype(BF16)
    zq, zk, zv = proj("qD"), proj("kD"), proj("vD")
    for hd in range(N_HEADS):
        sl = slice(hd * LANES, (hd + 1) * LANES)
        hi, mid, lo = _split3(posf * SLOPES_D[hd])
        alibi = jnp.where(lane == HEAD_DIM, hi.astype(F32),
                          jnp.where(lane == HEAD_DIM + 1, mid.astype(F32),
                                    jnp.where(lane == HEAD_DIM + 2, lo.astype(F32), 0.0)))
        qD_ref[0, :, sl] = (zq[:, sl] + ones3).astype(BF16)
        kD_ref[0, :, sl] = (zk[:, sl] + alibi + blk_hot).astype(BF16)
        vD_ref[0, :, sl] = (zv[:, sl] + one64).astype(BF16)

    zb = proj("mla")
    cq = zb[:, 0:256]
    ckv = zb[:, 256:384]
    kpm = zb[:, 384:512]
    kpr = zb[:, 512:640]
    cqn = cq * lax.rsqrt(jnp.sum(cq * cq, axis=-1, keepdims=True) * (1.0 / MLA_Q_RANK) + RMS_EPS) * qg_ref[...]
    ckvn = ckv * lax.rsqrt(jnp.mean(ckv * ckv, axis=-1, keepdims=True) + RMS_EPS) * kvg_ref[...]
    qall = _dot(cqn.astype(BF16), wq_ref[...])
    kvall = _dot(ckvn.astype(BF16), wkv_ref[...])
    cosq = cos_ref[...]
    sinq = sin_ref[...]
    kpe = kpm * cosq + kpr * sinq
    for hd in range(N_HEADS):
        lo = hd * 2 * LANES
        sl = slice(hd * LANES, (hd + 1) * LANES)
        qB_ref[0, :, sl] = (qall[:, lo:lo + LANES] * cosq + qall[:, lo + LANES:lo + 2 * LANES] * sinq).astype(BF16)
        kB_ref[0, :, sl] = (kvall[:, lo:lo + LANES] + kpe).astype(BF16)
        vB_ref[0, :, sl] = (kvall[:, lo + LANES:lo + 2 * LANES] + one64).astype(BF16)


def _in_proj(x, mod, g, wmain, qg, kvg, wq, wkv, cos_t, sin_t, ts=256):
    b, s, d = x.shape
    nt = s // ts
    tok = lambda w: pl.BlockSpec((1, ts, w), lambda i, j: (i, j, 0))
    full = lambda a: pl.BlockSpec(a.shape, lambda i, j: (0,) * a.ndim)
    outs = [("qB", 512, BF16), ("kB", 512, BF16), ("vB", 512, BF16), ("qA", 512, BF16), ("kA", 256, BF16),
            ("vA", 256, BF16), ("qC", 512, BF16), ("kcvc", 128, BF16), ("kvC", 640, BF16), ("gl", 128, F32),
            ("qD", 512, BF16), ("kD", 512, BF16), ("vD", 512, BF16)]
    res = pl.pallas_call(
        _in_proj_kernel,
        grid=(b, nt),
        in_specs=[tok(d), pl.BlockSpec((1, N_ADA, d), lambda i, j: (i, 0, 0)), full(g), full(wmain), full(qg),
                  full(kvg), full(wq), full(wkv),
                  pl.BlockSpec((ts, LANES), lambda i, j: (j, 0)), pl.BlockSpec((ts, LANES), lambda i, j: (j, 0))],
        out_specs=[tok(w) for _, w, _ in outs],
        out_shape=[jax.ShapeDtypeStruct((b, s, w), dt) for _, w, dt in outs],
        compiler_params=_cparams(("parallel", "parallel")),
        name="in_proj",
    )(x, mod, g, wmain, qg, kvg, wq, wkv, cos_t, sin_t)
    return {n: r for (n, _, _), r in zip(outs, res)}


def _fold_max(s):
    out = s[:, 0:LANES]
    for j in range(1, s.shape[1] // LANES):
        out = jnp.maximum(out, s[:, j * LANES:(j + 1) * LANES])
    return out


LOG2E = math.log2(math.e)


def _grouped_loops(n, groups, body, carry):
    pos = 0
    left = n
    for g in groups:
        cnt = left // g
        carry = lax.fori_loop(0, cnt, lambda i, cr, g=g, pos=pos: body(pos + i * g, g, cr), carry)
        pos = pos + cnt * g
        left = left - cnt * g
    return carry


def _two_pass(streams, n_past, tk, s_sc, scale=1.0, groups=(4, 2, 1), chunk_of=None):
    sc2 = scale * LOG2E

    def row0(c):
        return pl.multiple_of(c * tk, tk)

    def pass1(pos, g, mxs):
        out = []
        for i, st in enumerate(streams):
            if chunk_of is None:
                parts_src = [_dot_nt(st["q"], st["k_rows"](row0(pos), g * tk)) * sc2]
                parts = [parts_src[0][:, j * tk:(j + 1) * tk] for j in range(g)]
            else:
                parts = [_dot_nt(st["q"], st["k_rows"](row0(chunk_of(pos + j)), tk)) * sc2 for j in range(g)]
            mx = mxs[i]
            for j in range(g):
                s_sc[i, pos + j] = parts[j]
                mx = jnp.maximum(mx, _fold_max(parts[j]))
            out.append(mx)
        return tuple(out)

    own = [st["s_own"] * LOG2E for st in streams]
    mxs = _grouped_loops(n_past, groups, pass1, tuple(_fold_max(s) for s in own))
    m128 = [jnp.broadcast_to(jnp.max(mx, axis=1, keepdims=True), mx.shape) for mx in mxs]

    def p_of(s, mt):
        return jnp.exp2(s - jnp.concatenate([mt] * (s.shape[1] // LANES), axis=1)).astype(BF16)

    def pass2(pos, g, accs):
        out = []
        for i, st in enumerate(streams):
            if chunk_of is None:
                p = jnp.concatenate([p_of(s_sc[i, pos + j], m128[i]) for j in range(g)], axis=1)
                out.append(accs[i] + _dot(p, st["v_rows"](row0(pos), g * tk)))
            else:
                acc = accs[i]
                for j in range(g):
                    acc = acc + _dot(p_of(s_sc[i, pos + j], m128[i]), st["v_rows"](row0(chunk_of(pos + j)), tk))
                out.append(acc)
        return tuple(out)

    init = tuple(_dot(p_of(own[i], m128[i]), st["v_own"]) for i, st in enumerate(streams))
    return _grouped_loops(n_past, groups, pass2, init)


def _finish_ones(acc):
    return acc[:, :HEAD_DIM] * (1.0 / jnp.maximum(acc[:, HEAD_DIM:HEAD_DIM + 1], TINY))


def _dist(tq, tk, t0, k0):
    r = lax.broadcasted_iota(jnp.int32, (tq, tk), 0)
    c = lax.broadcasted_iota(jnp.int32, (tq, tk), 1)
    return (r - c) + (t0 - k0)


def _pack_heads(os):
    return jnp.concatenate([o[:, :HEAD_DIM] for o in os], axis=1)


SWA_TQ = 256


def _swa_kernel(sink_ref, q_ref, k_ref, v_ref, o_ref):
    t0 = pl.program_id(1) * SWA_TQ
    slab = SWA_WINDOW + SWA_TQ
    k0 = pl.multiple_of(jnp.maximum(t0 - SWA_WINDOW, 0), SWA_WINDOW)
    dist = _dist(SWA_TQ, slab, t0, k0)
    ok = jnp.logical_and(dist >= 0, dist < SWA_WINDOW)
    tpos = (lax.broadcasted_iota(jnp.int32, (SWA_TQ, 1), 0) + t0).astype(F32)
    outs = []
    for hd in range(N_HEADS):
        kv = hd // (N_HEADS // SWA_KV_HEADS)
        ksl = slice(kv * LANES, (kv + 1) * LANES)
        s = jnp.where(ok, _dot_nt(q_ref[0, :, hd * LANES:(hd + 1) * LANES], k_ref[0, pl.ds(k0, slab), ksl]), NEG)
        sink = sink_ref[hd] + SLOPES_A[hd] * tpos
        m = jnp.maximum(jnp.max(s, axis=1, keepdims=True), sink)
        acc = _dot(jnp.exp(s - m).astype(BF16), v_ref[0, pl.ds(k0, slab), ksl])
        den = acc[:, HEAD_DIM:HEAD_DIM + 1] + jnp.exp(sink - m)
        outs.append(acc[:, :HEAD_DIM] * (1.0 / jnp.maximum(den, TINY)))
    o_ref[0] = jnp.concatenate(outs, axis=1).astype(o_ref.dtype)


def _swa(qA, kA, vA, sinks):
    b, s, _ = qA.shape
    assert s % SWA_TQ == 0 and s >= SWA_WINDOW + SWA_TQ
    tile = lambda w: pl.BlockSpec((1, SWA_TQ, w), lambda i, j: (i, j, 0))
    whole = lambda w: pl.BlockSpec((1, s, w), lambda i, j: (i, 0, 0))
    return pl.pallas_call(
        _swa_kernel,
        grid=(b, s // SWA_TQ),
        in_specs=[pl.BlockSpec(memory_space=pltpu.SMEM), tile(N_HEADS * LANES), whole(SWA_KV_HEADS * LANES),
                  whole(SWA_KV_HEADS * LANES)],
        out_specs=tile(N_HEADS * HEAD_DIM),
        out_shape=jax.ShapeDtypeStruct((b, s, N_HEADS * HEAD_DIM), BF16),
        compiler_params=_cparams(("parallel", "arbitrary")),
        name="swa",
    )(sinks, qA, kA, vA)


MLA_SCALE = float((MLA_NOPE + MLA_ROPE) ** -0.5)


def _rows_at(ref, sl):
    return lambda row0, n: ref[0, pl.ds(row0, n), sl]


def _mla_kernel(q_ref, k_ref, v_ref, o_ref, s_sc, *, tq, tk):
    t0 = pl.program_id(2) * tq
    n_past = t0 // tk
    k_own = pl.multiple_of(n_past * tk, tk)
    bias = jnp.where(_dist(tq, tk, t0, k_own) >= 0, 0.0, NEG)
    streams = []
    for hd in range(2):
        sl = slice(hd * LANES, (hd + 1) * LANES)
        q = q_ref[0, :, sl]
        s_own = _dot_nt(q, k_ref[0, pl.ds(k_own, tk), sl]) * MLA_SCALE + bias
        streams.append(dict(q=q, k_rows=_rows_at(k_ref, sl), v_rows=_rows_at(v_ref, sl), s_own=s_own,
                            v_own=v_ref[0, pl.ds(k_own, tk), sl]))
    accs = _two_pass(streams, n_past, tk, s_sc, scale=MLA_SCALE)
    o_ref[0] = jnp.concatenate([_finish_ones(acc) for acc in accs], axis=1).astype(o_ref.dtype)


def _mla(qB, kB, vB, tq=256, tk=512):
    b, s, _ = qB.shape
    tk = min(tk, s)
    nq = s // tq
    return pl.pallas_call(
        functools.partial(_mla_kernel, tq=tq, tk=tk),
        grid=(b, 2, nq),
        in_specs=[pl.BlockSpec((1, tq, 2 * LANES), lambda bi, p, i: (bi, i, p)),
                  pl.BlockSpec((1, s, 2 * LANES), lambda bi, p, i: (bi, 0, p)),
                  pl.BlockSpec((1, s, 2 * LANES), lambda bi, p, i: (bi, 0, p))],
        out_specs=pl.BlockSpec((1, tq, LANES), lambda bi, p, i: (bi, i, p)),
        out_shape=jax.ShapeDtypeStruct((b, s, N_HEADS * HEAD_DIM), BF16),
        scratch_shapes=[pltpu.VMEM((2, max(s // tk - 1, 1), tq, tk), F32)],
        compiler_params=_cparams(("parallel", "parallel", "arbitrary")),
        name="mla_attn",
    )(qB, kB, vB)


def _gelu_tanh(x):
    return x * (0.5 * (1.0 + jnp.tanh(math.sqrt(2.0 / math.pi) * (x + 0.044715 * (x * x * x)))))


def _nsa_cmp_kernel(ch_ref, w1_ref, pos_ref, w1f_ref, w2_ref, kc_ref, vc_ref):
    nc = ch_ref.shape[1]
    u = _dot(ch_ref[0], w1_ref[...])
    for idx, out in ((0, kc_ref), (1, vc_ref)):
        pos = jnp.broadcast_to(pos_ref[idx], (8, pos_ref.shape[2]))
        w1f = w1f_ref[idx]
        p_hi = pos.astype(BF16)
        p_lo = (pos - p_hi.astype(F32)).astype(BF16)
        w_hi = w1f.astype(BF16)
        w_lo = (w1f - w_hi.astype(F32)).astype(BF16)
        posb = (_dot(p_hi, w_hi) + _dot(p_hi, w_lo) + _dot(p_lo, w_hi))[0:1, :]
        top = u[:, (2 * idx) * LANES:(2 * idx + 1) * LANES]
        bot = u[:, (2 * idx + 1) * LANES:(2 * idx + 2) * LANES]
        hid = top + pltpu.roll(bot, nc - 1, 0) + posb
        res = _dot(_gelu_tanh(hid).astype(BF16), w2_ref[idx])
        if idx == 0:
            lane = lax.broadcasted_iota(jnp.int32, (nc, LANES), 1)
            cend = lax.broadcasted_iota(jnp.int32, (nc, LANES), 0) * NSA_CMP_STRIDE + (NSA_CMP_LEN - 1)
            res = res + _pos_lanes(lane, cend)
        out[0] = res.astype(out.dtype)


def _nsa_compress(ch, w1aug, pos2, w1f, w2p):
    b, nc, w = ch.shape
    full = lambda a: pl.BlockSpec(a.shape, lambda i: (0,) * a.ndim)
    return pl.pallas_call(
        _nsa_cmp_kernel,
        grid=(b,),
        in_specs=[pl.BlockSpec((1, nc, w), lambda i: (i, 0, 0)), full(w1aug), full(pos2), full(w1f), full(w2p)],
        out_specs=[pl.BlockSpec((1, nc, LANES), lambda i: (i, 0, 0))] * 2,
        out_shape=[jax.ShapeDtypeStruct((b, nc, LANES), BF16)] * 2,
        compiler_params=_cparams(("parallel",)),
        name="nsa_compress",
    )(ch, w1aug, pos2, w1f, w2p)


NSA_CHUNK = 256


def _dist_stacked(rows, tk, t0, k0):
    r = lax.broadcasted_iota(jnp.int32, (rows, tk), 0) & (QB - 1)
    c = lax.broadcasted_iota(jnp.int32, (rows, tk), 1)
    return (r - c) + (t0 - k0)


def _nsa_kernel(q_ref, kc_ref, vc_ref, kv_ref, gl_ref, mm_ref, o_ref, imp_sc, s_sc, act_sm):
    n = pl.program_id(1)
    t0 = n * QB
    nc = kc_ref.shape[1]
    nbl = LANES
    rows = N_HEADS * QB
    qh = [q_ref[0, :, hd * LANES:(hd + 1) * LANES] for hd in range(N_HEADS)]
    qs = jnp.concatenate(qh, axis=0)

    r = (lax.broadcasted_iota(jnp.int32, (rows, nc), 0) & (QB - 1)) + t0
    cidx = lax.broadcasted_iota(jnp.int32, (rows, nc), 1)
    ok = jnp.logical_and(r >= cidx * NSA_CMP_STRIDE + (NSA_CMP_LEN - 1), cidx < nc - 1)
    s = jnp.where(ok, _dot_nt(qs, kc_ref[0]), NEG)
    m = jnp.max(s, axis=1, keepdims=True)
    e = jnp.where(ok, jnp.exp(s - m), 0.0)
    p = e * (1.0 / jnp.maximum(jnp.sum(e, axis=1, keepdims=True), TINY))
    o_cmp = _dot(p.astype(BF16), vc_ref[0])
    psum = p[0:QB]
    for hd in range(1, N_HEADS):
        psum = psum + p[hd * QB:(hd + 1) * QB]

    mm = mm_ref[...]
    imp = None
    for piece in _split3(psum):
        d = _dot(piece, mm)
        imp = d if imp is None else imp + d
    imp_t = imp.T
    jrow = lax.broadcasted_iota(jnp.int32, (nbl, QB), 0)
    tcol = lax.broadcasted_iota(jnp.int32, (nbl, QB), 1) + t0
    cur = tcol >> 6
    cand = jrow <= cur
    forced = jnp.logical_or(jnp.logical_or(jrow == 0, jrow == cur), jrow == cur - 1)
    a = jnp.where(cand, jnp.where(forced, BIG, imp_t), NEG)
    imp_sc[...] = a
    n_cand = (t0 + QB - 1) // NSA_SEL_BLOCK + 1

    def rank_body(i, cnt):
        row = imp_sc[pl.ds(i, 1), :]
        c_ge = jnp.where(row >= a, 1.0, 0.0)
        c_gt = jnp.where(row > a, 1.0, 0.0)
        return cnt + jnp.where(jrow > i, c_ge, c_gt)

    rank = lax.fori_loop(0, n_cand, rank_body, jnp.zeros((nbl, QB), F32))
    sel_t = jnp.where(jnp.logical_and(rank < NSA_TOPN, cand), 0.0, NEG)
    selneg = sel_t.T.astype(BF16)

    q_sel = jnp.concatenate([jnp.concatenate([qh[hd], selneg], axis=1) for hd in range(N_HEADS)], axis=0)
    own = t0 // NSA_CHUNK
    k_own = pl.multiple_of(own * NSA_CHUNK, NSA_CHUNK)
    ksl, vsl = slice(0, 2 * LANES), slice(2 * LANES, 3 * LANES)
    s_own = (_dot_nt(q_sel, kv_ref[0, pl.ds(k_own, NSA_CHUNK), ksl])
             + jnp.where(_dist_stacked(rows, NSA_CHUNK, t0, k_own) >= 0, 0.0, NEG))
    n_chunks = kv_ref.shape[1] // NSA_CHUNK
    blk_shift = (NSA_CHUNK // NSA_SEL_BLOCK).bit_length() - 1
    grp = jnp.where((lax.broadcasted_iota(jnp.int32, (n_chunks, nbl), 1) >> blk_shift)
                    == lax.broadcasted_iota(jnp.int32, (n_chunks, nbl), 0), 1.0, 0.0).astype(BF16)
    picked = jnp.where(sel_t == 0.0, 1.0, 0.0).astype(BF16)
    cnt = _dot(_dot(grp, picked).astype(BF16), jnp.ones((QB, LANES), BF16))
    n_act = jnp.int32(0)
    for c in range(n_chunks):
        act_sm[n_act] = c
        n_act = n_act + jnp.where(jnp.logical_and(cnt[c, 0] > 0.0, c < own), 1, 0)
    stream = dict(q=q_sel, k_rows=_rows_at(kv_ref, ksl), v_rows=_rows_at(kv_ref, vsl),
                  s_own=s_own, v_own=kv_ref[0, pl.ds(k_own, NSA_CHUNK), vsl])
    o_sel = _finish_ones(_two_pass([stream], n_act, NSA_CHUNK, s_sc, groups=(2, 1),
                                   chunk_of=lambda pos: act_sm[pos])[0])

    slab = NSA_WINDOW + QB
    k0w = pl.multiple_of(jnp.maximum(n - NSA_WINDOW // QB, 0) * QB, QB)
    dist = _dist_stacked(rows, slab, t0, k0w)
    okw = jnp.logical_and(dist >= 0, dist < NSA_WINDOW)
    s = jnp.where(okw, _dot_nt(qs, kv_ref[0, pl.ds(k0w, slab), 3 * LANES:4 * LANES]), NEG)
    pw = jnp.exp(s - jnp.max(s, axis=1, keepdims=True))
    o_win = _finish_ones(_dot(pw.astype(BF16), kv_ref[0, pl.ds(k0w, slab), 4 * LANES:5 * LANES]))

    g = jax.nn.sigmoid(gl_ref[0])
    outs = []
    for hd in range(N_HEADS):
        rs = slice(hd * QB, (hd + 1) * QB)
        outs.append(g[:, hd:hd + 1] * o_cmp[rs, :HEAD_DIM] + g[:, N_HEADS + hd:N_HEADS + hd + 1] * o_sel[rs]
                    + g[:, 2 * N_HEADS + hd:2 * N_HEADS + hd + 1] * o_win[rs])
    o_ref[0] = jnp.concatenate(outs, axis=1).astype(o_ref.dtype)


def _nsa(qC, kcmp, vcmp, kvC, gl, m_mat):
    b, s, _ = qC.shape
    nq = s // QB
    nc = kcmp.shape[1]
    assert s >= NSA_WINDOW + QB and s % NSA_CHUNK == 0
    return pl.pallas_call(
        _nsa_kernel,
        grid=(b, nq),
        in_specs=[pl.BlockSpec((1, QB, N_HEADS * LANES), lambda i, j: (i, j, 0)),
                  pl.BlockSpec((1, nc, LANES), lambda i, j: (i, 0, 0)),
                  pl.BlockSpec((1, nc, LANES), lambda i, j: (i, 0, 0)),
                  pl.BlockSpec((1, s, 5 * LANES), lambda i, j: (i, 0, 0)),
                  pl.BlockSpec((1, QB, LANES), lambda i, j: (i, j, 0)),
                  pl.BlockSpec(m_mat.shape, lambda i, j: (0, 0))],
        out_specs=pl.BlockSpec((1, QB, N_HEADS * HEAD_DIM), lambda i, j: (i, j, 0)),
        out_shape=jax.ShapeDtypeStruct((b, s, N_HEADS * HEAD_DIM), BF16),
        scratch_shapes=[pltpu.VMEM((LANES, QB), F32),
                        pltpu.VMEM((1, max(s // NSA_CHUNK - 1, 1), N_HEADS * QB, NSA_CHUNK), F32),
                        pltpu.SMEM((s // NSA_CHUNK + 1,), jnp.int32)],
        compiler_params=_cparams(("parallel", "arbitrary")),
        name="nsa_attn",
    )(qC, kcmp, vcmp, kvC, gl, m_mat)


MOBA_NBR = LANES - MOBA_LANE0


def _moba_kernel(q_ref, k_ref, v_ref, o_ref, kmean_sc, gate_sc, s_sc):
    tq = MOBA_BLOCK
    cur = pl.program_id(2)
    t0 = cur * tq
    n_blk = k_ref.shape[1] // MOBA_BLOCK
    lane = lax.broadcasted_iota(jnp.int32, (1, LANES), 1)

    @pl.when(cur == 0)
    def _():
        kmean_sc[...] = jnp.zeros(kmean_sc.shape, F32)
        for hd in range(2):
            for jb in range(n_blk):
                blk = k_ref[0, jb * MOBA_BLOCK:(jb + 1) * MOBA_BLOCK, hd * LANES:(hd + 1) * LANES].astype(F32)
                kmean_sc[hd, jb:jb + 1, :] = jnp.where(lane < HEAD_DIM, jnp.mean(blk, axis=0, keepdims=True), 0.0)

    jrow = lax.broadcasted_iota(jnp.int32, (MOBA_NBR, tq), 0)
    qs = [q_ref[0, :, hd * LANES:(hd + 1) * LANES] for hd in range(2)]
    gates = []
    for hd in range(2):
        gate_t = _dot_nt(kmean_sc[hd].astype(BF16), qs[hd])[0:MOBA_NBR, :]
        a = jnp.where(jrow < cur, gate_t, NEG)
        gate_sc[hd] = a
        gates.append(a)

    def rank_body(i, cnts):
        out = []
        for hd in range(2):
            row = gate_sc[hd, pl.ds(i, 1), :]
            c_ge = jnp.where(row >= gates[hd], 1.0, 0.0)
            c_gt = jnp.where(row > gates[hd], 1.0, 0.0)
            out.append(cnts[hd] + jnp.where(jrow > i, c_ge, c_gt))
        return tuple(out)

    ranks = lax.fori_loop(0, cur, rank_body, tuple(jnp.zeros((MOBA_NBR, tq), F32) for _ in range(2)))
    tk = 2 * MOBA_BLOCK
    k_own = pl.multiple_of(t0, MOBA_BLOCK)
    bias = jnp.where(_dist(tq, MOBA_BLOCK, 0, 0) >= 0, 0.0, NEG)
    streams = []
    for hd in range(2):
        sl = slice(hd * LANES, (hd + 1) * LANES)
        sel_t = jnp.where(jnp.logical_and(ranks[hd] < MOBA_TOPK, jrow < cur), 0.0, NEG)
        full_t = jnp.concatenate([jnp.zeros((MOBA_LANE0, tq), F32), sel_t], axis=0)
        q_sel = (qs[hd].astype(F32) + full_t.T).astype(BF16)
        s_own = _dot_nt(qs[hd], k_ref[0, pl.ds(k_own, MOBA_BLOCK), sl]) + bias
        streams.append(dict(q=q_sel, k_rows=_rows_at(k_ref, sl), v_rows=_rows_at(v_ref, sl), s_own=s_own,
                            v_own=v_ref[0, pl.ds(k_own, MOBA_BLOCK), sl]))
    accs = _two_pass(streams, (cur + 1) // 2, tk, s_sc)
    o_ref[0] = jnp.concatenate([_finish_ones(acc) for acc in accs], axis=1).astype(o_ref.dtype)


def _moba(qD, kD, vD):
    b, s, _ = qD.shape
    tq = MOBA_BLOCK
    assert s % (2 * MOBA_BLOCK) == 0 and s // MOBA_BLOCK <= MOBA_NBR
    return pl.pallas_call(
        _moba_kernel,
        grid=(b, 2, s // tq),
        in_specs=[pl.BlockSpec((1, tq, 2 * LANES), lambda i, p, j: (i, j, p)),
                  pl.BlockSpec((1, s, 2 * LANES), lambda i, p, j: (i, 0, p)),
                  pl.BlockSpec((1, s, 2 * LANES), lambda i, p, j: (i, 0, p))],
        out_specs=pl.BlockSpec((1, tq, LANES), lambda i, p, j: (i, j, p)),
        out_shape=jax.ShapeDtypeStruct((b, s, N_HEADS * HEAD_DIM), BF16),
        scratch_shapes=[pltpu.VMEM((2, LANES, LANES), F32), pltpu.VMEM((2, MOBA_NBR, tq), F32),
                        pltpu.VMEM((2, s // (2 * MOBA_BLOCK), tq, 2 * MOBA_BLOCK), F32)],
        compiler_params=_cparams(("parallel", "parallel", "arbitrary")),
        name="moba_attn",
    )(qD, kD, vD)


def _post_kernel(oa_ref, ob_ref, oc_ref, od_ref, x_ref, mod_ref, wo_ref, g_ref, wu_ref, wd_ref, fg_ref, out_ref,
                 *, final, ffc):
    x = x_ref[0]
    mix = None
    for idx, ref in enumerate((oa_ref, ob_ref, oc_ref, od_ref)):
        d = _dot(ref[0], wo_ref[idx])
        mix = d if mix is None else mix + d
    x1 = x + mod_ref[0, 2:3, :] * mix
    y = x1 * lax.rsqrt(jnp.mean(x1 * x1, axis=-1, keepdims=True) + RMS_EPS) * g_ref[...]
    hb = (y * (1.0 + mod_ref[0, 4:5, :]) + mod_ref[0, 3:4, :]).astype(BF16)
    acc = None
    for c in range(wu_ref.shape[1] // ffc):
        hid = jnp.maximum(_dot(hb, wu_ref[:, c * ffc:(c + 1) * ffc]), 0.0)
        d = _dot((hid * hid).astype(BF16), wd_ref[c * ffc:(c + 1) * ffc, :])
        acc = d if acc is None else acc + d
    x2 = x1 + mod_ref[0, 5:6, :] * acc
    if final:
        x2 = x2 * lax.rsqrt(jnp.mean(x2 * x2, axis=-1, keepdims=True) + RMS_EPS) * fg_ref[...]
    out_ref[0] = x2


def _post(oa, ob, oc, od, x, mod, wo4, g, wu, wd, fg, final, ts=256, ffc=1024):
    b, s, d = x.shape
    nt = s // ts
    tok = lambda w: pl.BlockSpec((1, ts, w), lambda i, j: (i, j, 0))
    full = lambda a: pl.BlockSpec(a.shape, lambda i, j: (0,) * a.ndim, pipeline_mode=pl.Buffered(1))
    hw = N_HEADS * HEAD_DIM
    return pl.pallas_call(
        functools.partial(_post_kernel, final=final, ffc=ffc),
        grid=(b, nt),
        in_specs=[tok(hw), tok(hw), tok(hw), tok(hw), tok(d), pl.BlockSpec((1, N_ADA, d), lambda i, j: (i, 0, 0)),
                  full(wo4), full(g), full(wu), full(wd), full(fg)],
        out_specs=tok(d),
        out_shape=jax.ShapeDtypeStruct((b, s, d), F32),
        compiler_params=_cparams(("parallel", "parallel")),
        name="post_mlp",
    )(oa, ob, oc, od, x, mod, wo4, g, wu, wd, fg)


def _pad_heads(w, n_heads, scale=None):
    k = w.shape[0]
    w = w.reshape(k, n_heads, HEAD_DIM)
    if scale is not None:
        w = w * scale
    return jnp.pad(w, ((0, 0), (0, 0), (0, LANES - HEAD_DIM))).reshape(k, n_heads * LANES)


def _rot_half_cols(w):
    half = w.shape[1] // 2
    return jnp.concatenate([-w[:, half:], w[:, :half]], axis=1)


def _layer_weights(w_in, mla_qg, mla_kvg, w_uq, w_ukv, pos_k, pos_v, ck_w1, ck_w2, cv_w1, cv_w2):
    d = w_in.shape[0]
    z = lambda n: jnp.zeros((d, n), F32)
    o = 0

    def take(n):
        nonlocal o
        r = w_in[:, o:o + n]
        o += n
        return r

    qa, ka, va = take(256), take(128), take(128)
    cq, ckv, kpe = take(MLA_Q_RANK), take(MLA_KV_RANK), take(MLA_ROPE)
    qc, kc, vc, ks, vs, kw, vw, gl = take(256), take(64), take(64), take(64), take(64), take(64), take(64), take(12)
    qd, kd, vd = take(256), take(256), take(256)
    sc = 1.0 / 8.0
    kpe_main = jnp.concatenate([z(64), kpe, z(32)], axis=1)
    kpe_rot = jnp.concatenate([z(64), _rot_half_cols(kpe), z(32)], axis=1)
    p64 = lambda w: jnp.pad(w, ((0, 0), (0, LANES - w.shape[1])))
    groups = {
        "mla": jnp.concatenate([cq, z(64), ckv, kpe_main, kpe_rot], axis=1),
        "qA": _pad_heads(qa, 4, sc), "kA": _pad_heads(ka, 2), "vA": _pad_heads(va, 2),
        "qC": _pad_heads(qc, 4, sc), "kcvc": jnp.concatenate([kc, vc], axis=1),
        "kvC": jnp.concatenate([p64(ks), z(LANES), p64(vs), p64(kw), p64(vw)], axis=1), "gl": p64(gl),
        "qD": _pad_heads(qd, 4, sc), "kD": _pad_heads(kd, 4), "vD": _pad_heads(vd, 4),
    }
    wmain = jnp.concatenate([groups[n] for n, _ in _IN_GROUPS], axis=1).astype(BF16)

    dq = MLA_NOPE + MLA_ROPE
    wq_cols, wkv_cols = [], []
    for hd in range(N_HEADS):
        wh = w_uq[:, hd * dq:(hd + 1) * dq]
        zq = lambda n: jnp.zeros((MLA_Q_RANK, n), F32)
        wq_cols += [wh, zq(32), zq(64), _rot_half_cols(wh[:, MLA_NOPE:]), zq(32)]
        wk = w_ukv[:, hd * 128:(hd + 1) * 128]
        zk = jnp.zeros((MLA_KV_RANK, 64), F32)
        wkv_cols += [wk[:, :64], zk, wk[:, 64:], zk]
    wq = jnp.pad(jnp.concatenate(wq_cols, axis=1), ((0, 256 - MLA_Q_RANK), (0, 0))).astype(BF16)
    wkv = jnp.concatenate(wkv_cols, axis=1).astype(BF16)
    qg = jnp.pad(mla_qg, (0, 256 - MLA_Q_RANK)).reshape(1, 256)
    kvg = mla_kvg.reshape(1, MLA_KV_RANK)


    def aug(w1, is_v):
        w = w1.reshape(NSA_CMP_LEN, HEAD_DIM, NSA_CMP_HIDDEN)
        zz = jnp.zeros_like(w)
        w = jnp.concatenate([zz, w] if is_v else [w, zz], axis=1)
        w = w.reshape(NSA_CMP_LEN * LANES, NSA_CMP_HIDDEN)
        return w[:NSA_CMP_STRIDE * LANES], w[NSA_CMP_STRIDE * LANES:]

    kt, kb = aug(ck_w1, False)
    vt, vb = aug(cv_w1, True)
    w1aug = jnp.concatenate([kt, kb, vt, vb], axis=1).astype(BF16)
    pos2 = jnp.stack([pos_k.reshape(1, -1), pos_v.reshape(1, -1)])
    w1f = jnp.stack([ck_w1, cv_w1])
    w2p = jnp.stack([p64(ck_w2), p64(cv_w2)]).astype(BF16)
    return wmain, qg, kvg, wq, wkv, w1aug, pos2, w1f, w2p


def _rope_tables(s):
    half = MLA_ROPE // 2
    freqs = ROPE_THETA ** (-jnp.arange(half, dtype=F32) / half)
    ang = jnp.arange(s, dtype=F32)[:, None] * freqs[None, :]
    cos, sin = jnp.cos(ang), jnp.sin(ang)
    ones, zeros = jnp.ones((s, MLA_NOPE), F32), jnp.zeros((s, MLA_NOPE), F32)
    tail = jnp.zeros((s, LANES - MLA_NOPE - MLA_ROPE), F32)
    return (jnp.concatenate([ones, cos, cos, tail], axis=1), jnp.concatenate([zeros, sin, sin, tail], axis=1))


def _nsa_imp_matrix(s):
    n_sel = s // NSA_SEL_BLOCK
    nc = s // NSA_CMP_STRIDE
    assert n_sel <= LANES
    c = np.arange(nc)[:, None]
    j = np.arange(LANES)[None, :]
    m = ((c >= 4 * j - 1) & (c <= 4 * j + 3) & (c < nc - 1) & (j < n_sel)).astype(np.float32)
    return jnp.asarray(m, BF16)


def kernel(x, c, norm_mix_g, norm_mlp_g, w_ada, b_ada, w_in, w_out, swa_sinks, mla_q_norm_g, mla_kv_norm_g,
           mla_w_uq, mla_w_ukv, nsa_cmp_pos_k, nsa_cmp_pos_v, nsa_cmp_k_w1, nsa_cmp_k_w2, nsa_cmp_v_w1,
           nsa_cmp_v_w2, w_up, w_down, final_norm_g):
    b, s, d = x.shape
    depth = w_in.shape[0]
    c8 = jnp.pad(c, ((0, 8 - b), (0, 0)))
    mod_all = _ada(c8, w_ada, b_ada)
    cos_t, sin_t = _rope_tables(s)
    m_mat = _nsa_imp_matrix(s)
    fg = final_norm_g.reshape(1, d)
    for l in range(depth):
        mod = mod_all[l, :b].reshape(b, N_ADA, d)
        wmain, qg, kvg, wq, wkv, w1aug, pos2, w1f, w2p = _layer_weights(
            w_in[l], mla_q_norm_g[l], mla_kv_norm_g[l], mla_w_uq[l], mla_w_ukv[l], nsa_cmp_pos_k[l],
            nsa_cmp_pos_v[l], nsa_cmp_k_w1[l], nsa_cmp_k_w2[l], nsa_cmp_v_w1[l], nsa_cmp_v_w2[l])
        z = _in_proj(x, mod, norm_mix_g[l].reshape(1, d), wmain, qg, kvg, wq, wkv, cos_t, sin_t)
        o_a = _swa(z["qA"], z["kA"], z["vA"], swa_sinks[l])
        o_b = _mla(z["qB"], z["kB"], z["vB"])
        ch = z["kcvc"].reshape(b, s // NSA_CMP_STRIDE, NSA_CMP_STRIDE * LANES)
        kcmp, vcmp = _nsa_compress(ch, w1aug, pos2, w1f, w2p)
        o_c = _nsa(z["qC"], kcmp, vcmp, z["kvC"], z["gl"], m_mat)
        o_d = _moba(z["qD"], z["kD"], z["vD"])
        wo4 = w_out[l].astype(BF16).reshape(4, N_HEADS * HEAD_DIM, d)
        x = _post(o_a, o_b, o_c, o_d, x, mod, wo4, norm_mlp_g[l].reshape(1, d), w_up[l].astype(BF16),
                  w_down[l].astype(BF16), fg, final=(l == depth - 1))
    return x
```

```python
import functools
import math

import numpy as np
import jax
import jax.numpy as jnp
from jax import lax
from jax.experimental import pallas as pl
from jax.experimental.pallas import tpu as pltpu

F32 = jnp.float32
BF16 = jnp.bfloat16

D_MODEL = 1024
DEPTH = 2
HEAD_DIM = 64
LANES = 128
QB = 128
NEG = -1e30
TINY = 1e-30
BIG = 1e9
RMS_EPS = 1e-6
N_ADA = 6
N_HEADS = 4

SWA_KV_HEADS = 2
SWA_WINDOW = 128
MLA_Q_RANK = 192
MLA_KV_RANK = 128
MLA_NOPE = 64
MLA_ROPE = 32
ROPE_THETA = 10000.0
NSA_CMP_LEN = 32
NSA_CMP_STRIDE = 16
NSA_CMP_HIDDEN = 128
NSA_SEL_BLOCK = 64
NSA_SEL_SHIFT = 6
NSA_TOPN = 16
NSA_WINDOW = 512
NSA_N_BRANCH = 3
MOBA_BLOCK = 256
MOBA_SHIFT = 8
MOBA_TOPK = 3
MOBA_LANE0 = 96
D_FF = 4 * D_MODEL
N_ALIBI = 3 * N_HEADS

_S_ALL = [2.0 ** (-8.0 * (i + 1) / N_ALIBI) for i in range(N_ALIBI)]
SLOPES_A = [float(np.float32(v)) for v in _S_ALL[0::3]]
SLOPES_C = [float(np.float32(v)) for v in _S_ALL[1::3]]
SLOPES_D = [float(np.float32(v)) for v in _S_ALL[2::3]]

VMEM_LIMIT = 56 * 1024 * 1024

_NT = (((1,), (1,)), ((), ()))


def _cparams(sem):
    return pltpu.CompilerParams(dimension_semantics=sem, vmem_limit_bytes=VMEM_LIMIT)


def _split3(a):
    hi = a.astype(BF16)
    r1 = a - hi.astype(F32)
    mid = r1.astype(BF16)
    lo = (r1 - mid.astype(F32)).astype(BF16)
    return hi, mid, lo


def _dot(a, b):
    return jnp.dot(a, b, preferred_element_type=F32)


def _dot_nt(a, b):
    return lax.dot_general(a, b, _NT, preferred_element_type=F32)


def _split3_const(x):
    x = np.float32(x)
    hi = np.float32(np.asarray(x, dtype=BF16))
    mid = np.float32(np.asarray(np.float32(x - hi), dtype=BF16))
    lo = np.float32(np.asarray(np.float32(x - hi - mid), dtype=BF16))
    return float(hi), float(mid), float(lo)


def _pieces_lanes(lane, pieces, repeat):
    out = jnp.zeros(lane.shape, F32)
    for i, pc in enumerate(pieces):
        lo = HEAD_DIM + i * repeat
        out = jnp.where(jnp.logical_and(lane >= lo, lane < lo + repeat), pc, out)
    return out


def _pos_lanes(lane, pos):
    hi = ((pos >> 7) << 7).astype(F32)
    lo = (pos & 127).astype(F32)
    k = lane - HEAD_DIM
    return jnp.where(jnp.logical_and(k >= 0, k < 6), jnp.where((k & 1) == 0, hi, lo), 0.0)


def _ada_kernel(c_ref, w_ref, b_ref, o_ref):
    c = c_ref[...]
    a = c * jax.nn.sigmoid(c)
    w = w_ref[0]
    a_hi = a.astype(BF16)
    a_lo = (a - a_hi.astype(F32)).astype(BF16)
    w_hi = w.astype(BF16)
    w_lo = (w - w_hi.astype(F32)).astype(BF16)
    o_ref[0] = _dot(a_hi, w_hi) + _dot(a_hi, w_lo) + _dot(a_lo, w_hi) + b_ref[0]


def _ada(c8, w_ada, b_ada):
    depth, d, n = w_ada.shape
    nb = n // d
    return pl.pallas_call(
        _ada_kernel,
        grid=(depth, nb),
        in_specs=[
            pl.BlockSpec((8, d), lambda l, j: (0, 0)),
            pl.BlockSpec((1, d, d), lambda l, j: (l, 0, j)),
            pl.BlockSpec((1, 1, d), lambda l, j: (l, 0, j)),
        ],
        out_specs=pl.BlockSpec((1, 8, d), lambda l, j: (l, 0, j)),
        out_shape=jax.ShapeDtypeStruct((depth, 8, n), F32),
        compiler_params=_cparams(("parallel", "parallel")),
        name="ada_mod",
    )(c8, w_ada, b_ada.reshape(depth, 1, n))


_IN_GROUPS = (("mla", 640), ("A", 512), ("C", 768), ("D", 768))
_IN_OFFS = {}
_o = 0
for _n, _w in _IN_GROUPS:
    _IN_OFFS[_n] = (_o, _o + _w)
    _o += _w
IN_COLS = _o


def _in_proj_kernel(x_ref, mod_ref, g_ref, w_ref, qg_ref, kvg_ref, wq_ref, wkv_ref, cos_ref, sin_ref,
                    qB_ref, kB_ref, vB_ref, qA_ref, kA_ref, vA_ref, qC_ref, kcvc_ref, kvC_ref, gl_ref,
                    qD_ref, kD_ref, vD_ref):
    x = x_ref[0]
    y = x * lax.rsqrt(jnp.mean(x * x, axis=-1, keepdims=True) + RMS_EPS) * g_ref[...]
    h = y * (1.0 + mod_ref[0, 1:2, :]) + mod_ref[0, 0:1, :]
    hb = h.astype(BF16)

    def proj(name):
        a, b = _IN_OFFS[name]
        return _dot(hb, w_ref[:, a:b])

    ts = x.shape[0]
    lane = lax.broadcasted_iota(jnp.int32, (ts, LANES), 1)
    pos = lax.broadcasted_iota(jnp.int32, (ts, LANES), 0) + pl.program_id(1) * ts
    low = lane < HEAD_DIM
    one64 = jnp.where(lane == HEAD_DIM, 1.0, 0.0)
    ones3 = jnp.where(jnp.logical_and(lane >= HEAD_DIM, lane < HEAD_DIM + 3), 1.0, 0.0)
    blk_hot = jnp.where(lane - MOBA_LANE0 == (pos >> MOBA_SHIFT), 1.0, 0.0)
    posf = pos.astype(F32)
    pos_lanes = _pos_lanes(lane, pos)

    def head_tile(z, hd, extra):
        pair = z[:, (hd // 2) * LANES:(hd // 2 + 1) * LANES]
        if hd % 2:
            pair = pltpu.roll(pair, HEAD_DIM, 1)
        return jnp.where(low, pair, extra).astype(BF16)

    za = proj("A")
    for hd in range(N_HEADS):
        qA_ref[0, :, hd * LANES:(hd + 1) * LANES] = head_tile(
            za, hd, _pieces_lanes(lane, _split3_const(SLOPES_A[hd]), repeat=2))
    for kv in range(SWA_KV_HEADS):
        sl = slice(kv * LANES, (kv + 1) * LANES)
        kA_ref[0, :, sl] = head_tile(za, N_HEADS + kv, pos_lanes)
        vA_ref[0, :, sl] = head_tile(za, N_HEADS + SWA_KV_HEADS + kv, one64)

    zc = proj("C")
    for hd in range(N_HEADS):
        qC_ref[0, :, hd * LANES:(hd + 1) * LANES] = head_tile(
            zc, hd, _pieces_lanes(lane, _split3_const(SLOPES_C[hd]), repeat=2))
    kcvc_ref[0] = zc[:, 2 * LANES:3 * LANES].astype(BF16)
    kvC_ref[0, :, 0:LANES] = head_tile(zc, 6, pos_lanes)
    kvC_ref[0, :, LANES:2 * LANES] = jnp.where(lane == (pos >> NSA_SEL_SHIFT), 1.0, 0.0).astype(BF16)
    kvC_ref[0, :, 2 * LANES:3 * LANES] = head_tile(zc, 7, one64)
    kvC_ref[0, :, 3 * LANES:4 * LANES] = head_tile(zc, 8, pos_lanes)
    kvC_ref[0, :, 4 * LANES:5 * LANES] = head_tile(zc, 9, one64)
    gl_ref[0] = zc[:, 5 * LANES:6 * LANES]

    zd = proj("D")
    for hd in range(N_HEADS):
        sl = slice(hd * LANES, (hd + 1) * LANES)
        hi, mid, lo = _split3(posf * SLOPES_D[hd])
        alibi = jnp.where(lane == HEAD_DIM, hi.astype(F32),
                          jnp.where(lane == HEAD_DIM + 1, mid.astype(F32),
                                    jnp.where(lane == HEAD_DIM + 2, lo.astype(F32), 0.0)))
        qD_ref[0, :, sl] = head_tile(zd, hd, ones3)
        kD_ref[0, :, sl] = head_tile(zd, N_HEADS + hd, alibi + blk_hot)
        vD_ref[0, :, sl] = head_tile(zd, 2 * N_HEADS + hd, one64)

    zb = proj("mla")
    cq = zb[:, 0:256]
    ckv = zb[:, 256:384]
    kpm = zb[:, 384:512]
    kpr = zb[:, 512:640]
    cqn = cq * lax.rsqrt(jnp.sum(cq * cq, axis=-1, keepdims=True) * (1.0 / MLA_Q_RANK) + RMS_EPS) * qg_ref[...]
    ckvn = ckv * lax.rsqrt(jnp.mean(ckv * ckv, axis=-1, keepdims=True) + RMS_EPS) * kvg_ref[...]
    qall = _dot(cqn.astype(BF16), wq_ref[...])
    kvall = _dot(ckvn.astype(BF16), wkv_ref[...])
    cosq = cos_ref[...]
    sinq = sin_ref[...]
    kpe = kpm * cosq + kpr * sinq
    for hd in range(N_HEADS):
        lo = hd * 2 * LANES
        sl = slice(hd * LANES, (hd + 1) * LANES)
        qB_ref[0, :, sl] = (qall[:, lo:lo + LANES] * cosq + qall[:, lo + LANES:lo + 2 * LANES] * sinq).astype(BF16)
        kB_ref[0, :, sl] = (kvall[:, lo:lo + LANES] + kpe).astype(BF16)
        vB_ref[0, :, sl] = (kvall[:, lo + LANES:lo + 2 * LANES] + one64).astype(BF16)


def _in_proj(x, mod, g, wmain, qg, kvg, wq, wkv, cos_t, sin_t, ts=256):
    b, s, d = x.shape
    nt = s // ts
    tok = lambda w: pl.BlockSpec((1, ts, w), lambda i, j: (i, j, 0))
    full = lambda a: pl.BlockSpec(a.shape, lambda i, j: (0,) * a.ndim)
    outs = [("qB", 512, BF16), ("kB", 512, BF16), ("vB", 512, BF16), ("qA", 512, BF16), ("kA", 256, BF16),
            ("vA", 256, BF16), ("qC", 512, BF16), ("kcvc", 128, BF16), ("kvC", 640, BF16), ("gl", 128, F32),
            ("qD", 512, BF16), ("kD", 512, BF16), ("vD", 512, BF16)]
    res = pl.pallas_call(
        _in_proj_kernel,
        grid=(b, nt),
        in_specs=[tok(d), pl.BlockSpec((1, N_ADA, d), lambda i, j: (i, 0, 0)), full(g), full(wmain), full(qg),
                  full(kvg), full(wq), full(wkv),
                  pl.BlockSpec((ts, LANES), lambda i, j: (j, 0)), pl.BlockSpec((ts, LANES), lambda i, j: (j, 0))],
        out_specs=[tok(w) for _, w, _ in outs],
        out_shape=[jax.ShapeDtypeStruct((b, s, w), dt) for _, w, dt in outs],
        compiler_params=_cparams(("parallel", "parallel")),
        name="in_proj",
    )(x, mod, g, wmain, qg, kvg, wq, wkv, cos_t, sin_t)
    return {n: r for (n, _, _), r in zip(outs, res)}


def _fold_max(s):
    out = s[:, 0:LANES]
    for j in range(1, s.shape[1] // LANES):
        out = jnp.maximum(out, s[:, j * LANES:(j + 1) * LANES])
    return out


LOG2E = math.log2(math.e)


def _grouped_loops(n, groups, body, carry):
    pos = 0
    left = n
    for g in groups:
        cnt = left // g
        carry = lax.fori_loop(0, cnt, lambda i, cr, g=g, pos=pos: body(pos + i * g, g, cr), carry)
        pos = pos + cnt * g
        left = left - cnt * g
    return carry


def _two_pass(streams, n_past, tk, s_sc, scale=1.0, groups=(4, 2, 1), chunk_of=None):
    sc2 = scale * LOG2E

    def row0(c):
        return pl.multiple_of(c * tk, tk)

    def pass1(pos, g, mxs):
        out = []
        for i, st in enumerate(streams):
            if chunk_of is None:
                parts_src = [_dot_nt(st["q"], st["k_rows"](row0(pos), g * tk)) * sc2]
                parts = [parts_src[0][:, j * tk:(j + 1) * tk] for j in range(g)]
            else:
                parts = [_dot_nt(st["q"], st["k_rows"](row0(chunk_of(pos + j)), tk)) * sc2 for j in range(g)]
            mx = mxs[i]
            for j in range(g):
                s_sc[i, pos + j] = parts[j]
                mx = jnp.maximum(mx, _fold_max(parts[j]))
            out.append(mx)
        return tuple(out)

    own = [st["s_own"] * LOG2E for st in streams]
    mxs = _grouped_loops(n_past, groups, pass1, tuple(_fold_max(s) for s in own))
    m128 = [jnp.broadcast_to(jnp.max(mx, axis=1, keepdims=True), mx.shape) for mx in mxs]

    def p_of(s, mt):
        return jnp.exp2(s - jnp.concatenate([mt] * (s.shape[1] // LANES), axis=1)).astype(BF16)

    def pass2(pos, g, accs):
        out = []
        for i, st in enumerate(streams):
            if chunk_of is None:
                p = jnp.concatenate([p_of(s_sc[i, pos + j], m128[i]) for j in range(g)], axis=1)
                out.append(accs[i] + _dot(p, st["v_rows"](row0(pos), g * tk)))
            else:
                acc = accs[i]
                for j in range(g):
                    acc = acc + _dot(p_of(s_sc[i, pos + j], m128[i]), st["v_rows"](row0(chunk_of(pos + j)), tk))
                out.append(acc)
        return tuple(out)

    init = tuple(_dot(p_of(own[i], m128[i]), st["v_own"]) for i, st in enumerate(streams))
    return _grouped_loops(n_past, groups, pass2, init)


def _finish_ones(acc):
    return acc[:, :HEAD_DIM] * (1.0 / jnp.maximum(acc[:, HEAD_DIM:HEAD_DIM + 1], TINY))


def _dist(tq, tk, t0, k0):
    r = lax.broadcasted_iota(jnp.int32, (tq, tk), 0)
    c = lax.broadcasted_iota(jnp.int32, (tq, tk), 1)
    return (r - c) + (t0 - k0)


SWA_TQ = 256


def _swa_kernel(sink_ref, q_ref, k_ref, v_ref, o_ref):
    t0 = pl.program_id(1) * SWA_TQ
    slab = SWA_WINDOW + SWA_TQ
    k0 = pl.multiple_of(jnp.maximum(t0 - SWA_WINDOW, 0), SWA_WINDOW)
    dist = _dist(SWA_TQ, slab, t0, k0)
    ok = jnp.logical_and(dist >= 0, dist < SWA_WINDOW)
    tpos = (lax.broadcasted_iota(jnp.int32, (SWA_TQ, 1), 0) + t0).astype(F32)
    outs = []
    for hd in range(N_HEADS):
        kv = hd // (N_HEADS // SWA_KV_HEADS)
        ksl = slice(kv * LANES, (kv + 1) * LANES)
        s = jnp.where(ok, _dot_nt(q_ref[0, :, hd * LANES:(hd + 1) * LANES], k_ref[0, pl.ds(k0, slab), ksl]), NEG)
        sink = sink_ref[hd] + SLOPES_A[hd] * tpos
        m = jnp.maximum(jnp.max(s, axis=1, keepdims=True), sink)
        acc = _dot(jnp.exp(s - m).astype(BF16), v_ref[0, pl.ds(k0, slab), ksl])
        den = acc[:, HEAD_DIM:HEAD_DIM + 1] + jnp.exp(sink - m)
        outs.append(acc[:, :HEAD_DIM] * (1.0 / jnp.maximum(den, TINY)))
    o_ref[0] = jnp.concatenate(outs, axis=1).astype(o_ref.dtype)


def _swa(qA, kA, vA, sinks):
    b, s, _ = qA.shape
    assert s % SWA_TQ == 0 and s >= SWA_WINDOW + SWA_TQ
    tile = lambda w: pl.BlockSpec((1, SWA_TQ, w), lambda i, j: (i, j, 0))
    whole = lambda w: pl.BlockSpec((1, s, w), lambda i, j: (i, 0, 0))
    return pl.pallas_call(
        _swa_kernel,
        grid=(b, s // SWA_TQ),
        in_specs=[pl.BlockSpec(memory_space=pltpu.SMEM), tile(N_HEADS * LANES), whole(SWA_KV_HEADS * LANES),
                  whole(SWA_KV_HEADS * LANES)],
        out_specs=tile(N_HEADS * HEAD_DIM),
        out_shape=jax.ShapeDtypeStruct((b, s, N_HEADS * HEAD_DIM), BF16),
        compiler_params=_cparams(("parallel", "arbitrary")),
        name="swa",
    )(sinks, qA, kA, vA)


MLA_SCALE = float((MLA_NOPE + MLA_ROPE) ** -0.5)


def _rows_at(ref, sl):
    return lambda row0, n: ref[0, pl.ds(row0, n), sl]


def _mla_kernel(q_ref, k_ref, v_ref, o_ref, s_sc, *, tq, tk):
    t0 = pl.program_id(2) * tq
    n_past = t0 // tk
    k_own = pl.multiple_of(n_past * tk, tk)
    bias = jnp.where(_dist(tq, tk, t0, k_own) >= 0, 0.0, NEG)
    streams = []
    for hd in range(2):
        sl = slice(hd * LANES, (hd + 1) * LANES)
        q = q_ref[0, :, sl]
        s_own = _dot_nt(q, k_ref[0, pl.ds(k_own, tk), sl]) * MLA_SCALE + bias
        streams.append(dict(q=q, k_rows=_rows_at(k_ref, sl), v_rows=_rows_at(v_ref, sl), s_own=s_own,
                            v_own=v_ref[0, pl.ds(k_own, tk), sl]))
    accs = _two_pass(streams, n_past, tk, s_sc, scale=MLA_SCALE)
    o_ref[0] = jnp.concatenate([_finish_ones(acc) for acc in accs], axis=1).astype(o_ref.dtype)


def _mla(qB, kB, vB, tq=512, tk=512):
    b, s, _ = qB.shape
    tk = min(tk, s)
    nq = s // tq
    resident = lambda: pl.BlockSpec((1, s, 2 * LANES), lambda bi, p, i: (bi, 0, p), pipeline_mode=pl.Buffered(1))
    return pl.pallas_call(
        functools.partial(_mla_kernel, tq=tq, tk=tk),
        grid=(b, 2, nq),
        in_specs=[pl.BlockSpec((1, tq, 2 * LANES), lambda bi, p, i: (bi, i, p)), resident(), resident()],
        out_specs=pl.BlockSpec((1, tq, LANES), lambda bi, p, i: (bi, i, p)),
        out_shape=jax.ShapeDtypeStruct((b, s, N_HEADS * HEAD_DIM), BF16),
        scratch_shapes=[pltpu.VMEM((2, max(s // tk - 1, 1), tq, tk), F32)],
        compiler_params=_cparams(("parallel", "parallel", "arbitrary")),
        name="mla_attn",
    )(qB, kB, vB)


def _gelu_tanh(x):
    return x * (0.5 * (1.0 + jnp.tanh(math.sqrt(2.0 / math.pi) * (x + 0.044715 * (x * x * x)))))


def _nsa_cmp_kernel(ch_ref, w1_ref, pos_ref, w1f_ref, w2_ref, kc_ref, vc_ref):
    nc = ch_ref.shape[1]
    u = _dot(ch_ref[0], w1_ref[...])
    for idx, out in ((0, kc_ref), (1, vc_ref)):
        pos = jnp.broadcast_to(pos_ref[idx], (8, pos_ref.shape[2]))
        w1f = w1f_ref[idx]
        p_hi = pos.astype(BF16)
        p_lo = (pos - p_hi.astype(F32)).astype(BF16)
        w_hi = w1f.astype(BF16)
        w_lo = (w1f - w_hi.astype(F32)).astype(BF16)
        posb = (_dot(p_hi, w_hi) + _dot(p_hi, w_lo) + _dot(p_lo, w_hi))[0:1, :]
        top = u[:, (2 * idx) * LANES:(2 * idx + 1) * LANES]
        bot = u[:, (2 * idx + 1) * LANES:(2 * idx + 2) * LANES]
        hid = top + pltpu.roll(bot, nc - 1, 0) + posb
        res = _dot(_gelu_tanh(hid).astype(BF16), w2_ref[idx])
        if idx == 0:
            lane = lax.broadcasted_iota(jnp.int32, (nc, LANES), 1)
            cend = lax.broadcasted_iota(jnp.int32, (nc, LANES), 0) * NSA_CMP_STRIDE + (NSA_CMP_LEN - 1)
            res = res + _pos_lanes(lane, cend)
        out[0] = res.astype(out.dtype)


def _nsa_compress(ch, w1aug, pos2, w1f, w2p):
    b, nc, w = ch.shape
    full = lambda a: pl.BlockSpec(a.shape, lambda i: (0,) * a.ndim)
    return pl.pallas_call(
        _nsa_cmp_kernel,
        grid=(b,),
        in_specs=[pl.BlockSpec((1, nc, w), lambda i: (i, 0, 0)), full(w1aug), full(pos2), full(w1f), full(w2p)],
        out_specs=[pl.BlockSpec((1, nc, LANES), lambda i: (i, 0, 0))] * 2,
        out_shape=[jax.ShapeDtypeStruct((b, nc, LANES), BF16)] * 2,
        compiler_params=_cparams(("parallel",)),
        name="nsa_compress",
    )(ch, w1aug, pos2, w1f, w2p)


NSA_CHUNK = 256


def _nsa_kernel(q_ref, kc_ref, vc_ref, kv_ref, gl_ref, mm_ref, o_ref, imp_sc, s_sc, act_sm):
    n = pl.program_id(1)
    t0 = n * QB
    nc = kc_ref.shape[1]
    nbl = LANES
    qh = [q_ref[0, :, hd * LANES:(hd + 1) * LANES] for hd in range(N_HEADS)]
    qs = jnp.concatenate(qh, axis=0)

    def stack(x):
        return jnp.concatenate([x] * N_HEADS, axis=0)

    slab = NSA_WINDOW + QB
    k0w = pl.multiple_of(jnp.maximum(n - NSA_WINDOW // QB, 0) * QB, QB)
    dist = _dist(QB, slab, t0, k0w)
    bias_w = stack(jnp.where(jnp.logical_and(dist >= 0, dist < NSA_WINDOW), 0.0, NEG))
    s = _dot_nt(qs, kv_ref[0, pl.ds(k0w, slab), 3 * LANES:4 * LANES]) + bias_w
    pw = jnp.exp(s - jnp.max(s, axis=1, keepdims=True))
    o_win = _finish_ones(_dot(pw.astype(BF16), kv_ref[0, pl.ds(k0w, slab), 4 * LANES:5 * LANES]))

    tq1 = lax.broadcasted_iota(jnp.int32, (QB, 1), 0) + t0
    cidx = lax.broadcasted_iota(jnp.int32, (QB, nc), 1)
    ok = jnp.logical_and(tq1 >= cidx * NSA_CMP_STRIDE + (NSA_CMP_LEN - 1), cidx < nc - 1)
    s = _dot_nt(qs, kc_ref[0]) + stack(jnp.where(ok, 0.0, NEG))
    e = jnp.exp(s - jnp.max(s, axis=1, keepdims=True))
    row_ok = stack(jnp.where(tq1 >= NSA_CMP_LEN - 1, 1.0, 0.0))
    p = e * (row_ok / jnp.maximum(jnp.sum(e, axis=1, keepdims=True), TINY))
    o_cmp = _dot(p.astype(BF16), vc_ref[0])
    psum = p[0:QB]
    for hd in range(1, N_HEADS):
        psum = psum + p[hd * QB:(hd + 1) * QB]

    mm = mm_ref[...]
    imp = None
    for piece in _split3(psum):
        d = _dot(piece, mm)
        imp = d if imp is None else imp + d
    imp_t = imp.T
    jrow = lax.broadcasted_iota(jnp.int32, (nbl, QB), 0)
    tcol = lax.broadcasted_iota(jnp.int32, (nbl, QB), 1) + t0
    cur = tcol >> 6
    cand = jrow <= cur
    forced = jnp.logical_or(jnp.logical_or(jrow == 0, jrow == cur), jrow == cur - 1)
    a = jnp.where(cand, jnp.where(forced, BIG, imp_t), NEG)
    imp_sc[...] = a
    n_cand = (t0 + QB - 1) // NSA_SEL_BLOCK + 1

    def rank_body(i2, cnt):
        for u in range(2):
            i = 2 * i2 + u
            row = imp_sc[pl.ds(i, 1), :]
            c_ge = jnp.where(row >= a, 1.0, 0.0)
            c_gt = jnp.where(row > a, 1.0, 0.0)
            cnt = cnt + jnp.where(jrow > i, c_ge, c_gt)
        return cnt

    rank = lax.fori_loop(0, n_cand // 2, rank_body, jnp.zeros((nbl, QB), F32))
    sel_t = jnp.where(jnp.logical_and(rank < NSA_TOPN, cand), 0.0, NEG)
    selneg = sel_t.T.astype(BF16)

    q_sel = jnp.concatenate([jnp.concatenate([qh[hd], selneg], axis=1) for hd in range(N_HEADS)], axis=0)
    own = t0 // NSA_CHUNK
    k_own = pl.multiple_of(own * NSA_CHUNK, NSA_CHUNK)
    ksl, vsl = slice(0, 2 * LANES), slice(2 * LANES, 3 * LANES)
    s_own = (_dot_nt(q_sel, kv_ref[0, pl.ds(k_own, NSA_CHUNK), ksl])
             + stack(jnp.where(_dist(QB, NSA_CHUNK, t0, k_own) >= 0, 0.0, NEG)))
    n_chunks = kv_ref.shape[1] // NSA_CHUNK
    blk_shift = (NSA_CHUNK // NSA_SEL_BLOCK).bit_length() - 1
    grp = jnp.where((lax.broadcasted_iota(jnp.int32, (n_chunks, nbl), 1) >> blk_shift)
                    == lax.broadcasted_iota(jnp.int32, (n_chunks, nbl), 0), 1.0, 0.0).astype(BF16)
    picked = jnp.where(sel_t == 0.0, 1.0, 0.0).astype(BF16)
    cnt = _dot(_dot(grp, picked).astype(BF16), jnp.ones((QB, LANES), BF16))
    n_act = jnp.int32(0)
    for c in range(n_chunks):
        act_sm[n_act] = c
        n_act = n_act + jnp.where(jnp.logical_and(cnt[c, 0] > 0.0, c < own), 1, 0)
    stream = dict(q=q_sel, k_rows=_rows_at(kv_ref, ksl), v_rows=_rows_at(kv_ref, vsl),
                  s_own=s_own, v_own=kv_ref[0, pl.ds(k_own, NSA_CHUNK), vsl])
    o_sel = _finish_ones(_two_pass([stream], n_act, NSA_CHUNK, s_sc, groups=(2, 1),
                                   chunk_of=lambda pos: act_sm[pos])[0])

    g = jax.nn.sigmoid(gl_ref[0])
    outs = []
    for hd in range(N_HEADS):
        rs = slice(hd * QB, (hd + 1) * QB)
        outs.append(g[:, hd:hd + 1] * o_cmp[rs, :HEAD_DIM] + g[:, N_HEADS + hd:N_HEADS + hd + 1] * o_sel[rs]
                    + g[:, 2 * N_HEADS + hd:2 * N_HEADS + hd + 1] * o_win[rs])
    o_ref[0] = jnp.concatenate(outs, axis=1).astype(o_ref.dtype)


def _nsa(qC, kcmp, vcmp, kvC, gl, m_mat):
    b, s, _ = qC.shape
    nq = s // QB
    nc = kcmp.shape[1]
    assert s >= NSA_WINDOW + QB and s % NSA_CHUNK == 0
    return pl.pallas_call(
        _nsa_kernel,
        grid=(b, nq),
        in_specs=[pl.BlockSpec((1, QB, N_HEADS * LANES), lambda i, j: (i, j, 0)),
                  pl.BlockSpec((1, nc, LANES), lambda i, j: (i, 0, 0)),
                  pl.BlockSpec((1, nc, LANES), lambda i, j: (i, 0, 0)),
                  pl.BlockSpec((1, s, 5 * LANES), lambda i, j: (i, 0, 0)),
                  pl.BlockSpec((1, QB, LANES), lambda i, j: (i, j, 0)),
                  pl.BlockSpec(m_mat.shape, lambda i, j: (0, 0))],
        out_specs=pl.BlockSpec((1, QB, N_HEADS * HEAD_DIM), lambda i, j: (i, j, 0)),
        out_shape=jax.ShapeDtypeStruct((b, s, N_HEADS * HEAD_DIM), BF16),
        scratch_shapes=[pltpu.VMEM((LANES, QB), F32),
                        pltpu.VMEM((1, max(s // NSA_CHUNK - 1, 1), N_HEADS * QB, NSA_CHUNK), F32),
                        pltpu.SMEM((s // NSA_CHUNK + 1,), jnp.int32)],
        compiler_params=_cparams(("parallel", "arbitrary")),
        name="nsa_attn",
    )(qC, kcmp, vcmp, kvC, gl, m_mat)


MOBA_NBR = LANES - MOBA_LANE0


MOBA_TQ = 2 * MOBA_BLOCK


def _moba_kernel(q_ref, k_ref, v_ref, o_ref, kmean_sc, gate_sc, s_sc):
    tq = MOBA_TQ
    n = pl.program_id(2)
    t0 = n * tq
    n_blk = k_ref.shape[1] // MOBA_BLOCK
    lane = lax.broadcasted_iota(jnp.int32, (1, LANES), 1)

    @pl.when(n == 0)
    def _():
        kmean_sc[...] = jnp.zeros(kmean_sc.shape, F32)
        for hd in range(2):
            for jb in range(n_blk):
                blk = k_ref[0, jb * MOBA_BLOCK:(jb + 1) * MOBA_BLOCK, hd * LANES:(hd + 1) * LANES].astype(F32)
                kmean_sc[hd, jb:jb + 1, :] = jnp.where(lane < HEAD_DIM, jnp.mean(blk, axis=0, keepdims=True), 0.0)

    jrow = lax.broadcasted_iota(jnp.int32, (MOBA_NBR, tq), 0)
    qcol = lax.broadcasted_iota(jnp.int32, (MOBA_NBR, tq), 1)
    cur = 2 * n + (qcol >> MOBA_SHIFT)
    qs = [q_ref[0, :, hd * LANES:(hd + 1) * LANES] for hd in range(2)]
    gates = []
    for hd in range(2):
        gate_t = _dot_nt(kmean_sc[hd].astype(BF16), qs[hd])[0:MOBA_NBR, :]
        a = jnp.where(jrow < cur, gate_t, NEG)
        gate_sc[hd] = a
        gates.append(a)

    def rank_body(i, cnts):
        out = []
        for hd in range(2):
            row = gate_sc[hd, pl.ds(i, 1), :]
            c_ge = jnp.where(row >= gates[hd], 1.0, 0.0)
            c_gt = jnp.where(row > gates[hd], 1.0, 0.0)
            out.append(cnts[hd] + jnp.where(jrow > i, c_ge, c_gt))
        return tuple(out)

    ranks = lax.fori_loop(0, 2 * n + 1, rank_body, tuple(jnp.zeros((MOBA_NBR, tq), F32) for _ in range(2)))
    tk = MOBA_TQ
    k_own = pl.multiple_of(t0, tk)
    bias = jnp.where(_dist(tq, tk, 0, 0) >= 0, 0.0, NEG)
    streams = []
    for hd in range(2):
        sl = slice(hd * LANES, (hd + 1) * LANES)
        keep = jnp.logical_or(jnp.logical_and(ranks[hd] < MOBA_TOPK, jrow < cur), jrow == cur)
        full_t = jnp.concatenate([jnp.zeros((MOBA_LANE0, tq), F32), jnp.where(keep, 0.0, NEG)], axis=0)
        q_sel = (qs[hd].astype(F32) + full_t.T).astype(BF16)
        s_own = _dot_nt(q_sel, k_ref[0, pl.ds(k_own, tk), sl]) + bias
        streams.append(dict(q=q_sel, k_rows=_rows_at(k_ref, sl), v_rows=_rows_at(v_ref, sl), s_own=s_own,
                            v_own=v_ref[0, pl.ds(k_own, tk), sl]))
    accs = _two_pass(streams, n, tk, s_sc)
    o_ref[0] = jnp.concatenate([_finish_ones(acc) for acc in accs], axis=1).astype(o_ref.dtype)


def _moba(qD, kD, vD):
    b, s, _ = qD.shape
    tq = MOBA_TQ
    assert s % tq == 0 and s // MOBA_BLOCK <= MOBA_NBR
    resident = lambda: pl.BlockSpec((1, s, 2 * LANES), lambda i, p, j: (i, 0, p), pipeline_mode=pl.Buffered(1))
    return pl.pallas_call(
        _moba_kernel,
        grid=(b, 2, s // tq),
        in_specs=[pl.BlockSpec((1, tq, 2 * LANES), lambda i, p, j: (i, j, p)), resident(), resident()],
        out_specs=pl.BlockSpec((1, tq, LANES), lambda i, p, j: (i, j, p)),
        out_shape=jax.ShapeDtypeStruct((b, s, N_HEADS * HEAD_DIM), BF16),
        scratch_shapes=[pltpu.VMEM((2, LANES, LANES), F32), pltpu.VMEM((2, MOBA_NBR, tq), F32),
                        pltpu.VMEM((2, max(s // tq - 1, 1), tq, tq), F32)],
        compiler_params=_cparams(("parallel", "parallel", "arbitrary")),
        name="moba_attn",
    )(qD, kD, vD)


def _post_kernel(oa_ref, ob_ref, oc_ref, od_ref, x_ref, mod_ref, wo_ref, g_ref, wu_ref, wd_ref, fg_ref, out_ref,
                 *, final, ffc):
    x = x_ref[0]
    mix = None
    for idx, ref in enumerate((oa_ref, ob_ref, oc_ref, od_ref)):
        d = _dot(ref[0], wo_ref[idx])
        mix = d if mix is None else mix + d
    x1 = x + mod_ref[0, 2:3, :] * mix
    y = x1 * lax.rsqrt(jnp.mean(x1 * x1, axis=-1, keepdims=True) + RMS_EPS) * g_ref[...]
    hb = (y * (1.0 + mod_ref[0, 4:5, :]) + mod_ref[0, 3:4, :]).astype(BF16)
    acc = None
    for c in range(wu_ref.shape[1] // ffc):
        hid = jnp.maximum(_dot(hb, wu_ref[:, c * ffc:(c + 1) * ffc]), 0.0)
        d = _dot((hid * hid).astype(BF16), wd_ref[c * ffc:(c + 1) * ffc, :])
        acc = d if acc is None else acc + d
    x2 = x1 + mod_ref[0, 5:6, :] * acc
    if final:
        x2 = x2 * lax.rsqrt(jnp.mean(x2 * x2, axis=-1, keepdims=True) + RMS_EPS) * fg_ref[...]
    out_ref[0] = x2


def _post(oa, ob, oc, od, x, mod, wo4, g, wu, wd, fg, final, ts=256, ffc=1024):
    b, s, d = x.shape
    nt = s // ts
    tok = lambda w: pl.BlockSpec((1, ts, w), lambda i, j: (i, j, 0))
    full = lambda a: pl.BlockSpec(a.shape, lambda i, j: (0,) * a.ndim, pipeline_mode=pl.Buffered(1))
    hw = N_HEADS * HEAD_DIM
    return pl.pallas_call(
        functools.partial(_post_kernel, final=final, ffc=ffc),
        grid=(b, nt),
        in_specs=[tok(hw), tok(hw), tok(hw), tok(hw), tok(d), pl.BlockSpec((1, N_ADA, d), lambda i, j: (i, 0, 0)),
                  full(wo4), full(g), full(wu), full(wd), full(fg)],
        out_specs=tok(d),
        out_shape=jax.ShapeDtypeStruct((b, s, d), F32),
        compiler_params=_cparams(("parallel", "parallel")),
        name="post_mlp",
    )(oa, ob, oc, od, x, mod, wo4, g, wu, wd, fg)


def _rot_half_cols(w):
    half = w.shape[1] // 2
    return jnp.concatenate([-w[:, half:], w[:, :half]], axis=1)


def _layer_weights(w_in, mla_qg, mla_kvg, w_uq, w_ukv, pos_k, pos_v, ck_w1, ck_w2, cv_w1, cv_w2):
    d = w_in.shape[0]
    z = lambda n: jnp.zeros((d, n), F32)
    o = 0

    def take(n):
        nonlocal o
        r = w_in[:, o:o + n]
        o += n
        return r

    qa, ka, va = take(256), take(128), take(128)
    cq, ckv, kpe = take(MLA_Q_RANK), take(MLA_KV_RANK), take(MLA_ROPE)
    qc, kc, vc, ks, vs, kw, vw, gl = take(256), take(64), take(64), take(64), take(64), take(64), take(64), take(12)
    qd, kd, vd = take(256), take(256), take(256)
    sc = 1.0 / 8.0
    kpe_main = jnp.concatenate([z(64), kpe, z(32)], axis=1)
    kpe_rot = jnp.concatenate([z(64), _rot_half_cols(kpe), z(32)], axis=1)
    p64 = lambda w: jnp.pad(w, ((0, 0), (0, LANES - w.shape[1])))
    groups = {
        "mla": jnp.concatenate([cq, z(64), ckv, kpe_main, kpe_rot], axis=1),
        "A": jnp.concatenate([qa * sc, ka, va], axis=1),
        "C": jnp.concatenate([qc * sc, kc, vc, ks, vs, kw, vw, p64(gl)], axis=1),
        "D": jnp.concatenate([qd * sc, kd, vd], axis=1),
    }
    wmain = jnp.concatenate([groups[n] for n, _ in _IN_GROUPS], axis=1).astype(BF16)

    dq = MLA_NOPE + MLA_ROPE
    wq_cols, wkv_cols = [], []
    for hd in range(N_HEADS):
        wh = w_uq[:, hd * dq:(hd + 1) * dq]
        zq = lambda n: jnp.zeros((MLA_Q_RANK, n), F32)
        wq_cols += [wh, zq(32), zq(64), _rot_half_cols(wh[:, MLA_NOPE:]), zq(32)]
        wk = w_ukv[:, hd * 128:(hd + 1) * 128]
        zk = jnp.zeros((MLA_KV_RANK, 64), F32)
        wkv_cols += [wk[:, :64], zk, wk[:, 64:], zk]
    wq = jnp.pad(jnp.concatenate(wq_cols, axis=1), ((0, 256 - MLA_Q_RANK), (0, 0))).astype(BF16)
    wkv = jnp.concatenate(wkv_cols, axis=1).astype(BF16)
    qg = jnp.pad(mla_qg, (0, 256 - MLA_Q_RANK)).reshape(1, 256)
    kvg = mla_kvg.reshape(1, MLA_KV_RANK)


    def aug(w1, is_v):
        w = w1.reshape(NSA_CMP_LEN, HEAD_DIM, NSA_CMP_HIDDEN)
        zz = jnp.zeros_like(w)
        w = jnp.concatenate([zz, w] if is_v else [w, zz], axis=1)
        w = w.reshape(NSA_CMP_LEN * LANES, NSA_CMP_HIDDEN)
        return w[:NSA_CMP_STRIDE * LANES], w[NSA_CMP_STRIDE * LANES:]

    kt, kb = aug(ck_w1, False)
    vt, vb = aug(cv_w1, True)
    w1aug = jnp.concatenate([kt, kb, vt, vb], axis=1).astype(BF16)
    pos2 = jnp.stack([pos_k.reshape(1, -1), pos_v.reshape(1, -1)])
    w1f = jnp.stack([ck_w1, cv_w1])
    w2p = jnp.stack([p64(ck_w2), p64(cv_w2)]).astype(BF16)
    return wmain, qg, kvg, wq, wkv, w1aug, pos2, w1f, w2p


def _rope_tables(s):
    half = MLA_ROPE // 2
    freqs = ROPE_THETA ** (-jnp.arange(half, dtype=F32) / half)
    ang = jnp.arange(s, dtype=F32)[:, None] * freqs[None, :]
    cos, sin = jnp.cos(ang), jnp.sin(ang)
    ones, zeros = jnp.ones((s, MLA_NOPE), F32), jnp.zeros((s, MLA_NOPE), F32)
    tail = jnp.zeros((s, LANES - MLA_NOPE - MLA_ROPE), F32)
    return (jnp.concatenate([ones, cos, cos, tail], axis=1), jnp.concatenate([zeros, sin, sin, tail], axis=1))


def _nsa_imp_matrix(s):
    n_sel = s // NSA_SEL_BLOCK
    nc = s // NSA_CMP_STRIDE
    assert n_sel <= LANES
    c = np.arange(nc)[:, None]
    j = np.arange(LANES)[None, :]
    m = ((c >= 4 * j - 1) & (c <= 4 * j + 3) & (c < nc - 1) & (j < n_sel)).astype(np.float32)
    return jnp.asarray(m, BF16)


def kernel(x, c, norm_mix_g, norm_mlp_g, w_ada, b_ada, w_in, w_out, swa_sinks, mla_q_norm_g, mla_kv_norm_g,
           mla_w_uq, mla_w_ukv, nsa_cmp_pos_k, nsa_cmp_pos_v, nsa_cmp_k_w1, nsa_cmp_k_w2, nsa_cmp_v_w1,
           nsa_cmp_v_w2, w_up, w_down, final_norm_g):
    b, s, d = x.shape
    depth = w_in.shape[0]
    c8 = jnp.pad(c, ((0, 8 - b), (0, 0)))
    mod_all = _ada(c8, w_ada, b_ada)
    cos_t, sin_t = _rope_tables(s)
    m_mat = _nsa_imp_matrix(s)
    fg = final_norm_g.reshape(1, d)
    for l in range(depth):
        mod = mod_all[l, :b].reshape(b, N_ADA, d)
        wmain, qg, kvg, wq, wkv, w1aug, pos2, w1f, w2p = _layer_weights(
            w_in[l], mla_q_norm_g[l], mla_kv_norm_g[l], mla_w_uq[l], mla_w_ukv[l], nsa_cmp_pos_k[l],
            nsa_cmp_pos_v[l], nsa_cmp_k_w1[l], nsa_cmp_k_w2[l], nsa_cmp_v_w1[l], nsa_cmp_v_w2[l])
        z = _in_proj(x, mod, norm_mix_g[l].reshape(1, d), wmain, qg, kvg, wq, wkv, cos_t, sin_t)
        o_a = _swa(z["qA"], z["kA"], z["vA"], swa_sinks[l])
        o_b = _mla(z["qB"], z["kB"], z["vB"])
        ch = z["kcvc"].reshape(b, s // NSA_CMP_STRIDE, NSA_CMP_STRIDE * LANES)
        kcmp, vcmp = _nsa_compress(ch, w1aug, pos2, w1f, w2p)
        o_c = _nsa(z["qC"], kcmp, vcmp, z["kvC"], z["gl"], m_mat)
        o_d = _moba(z["qD"], z["kD"], z["vD"])
        wo4 = w_out[l].astype(BF16).reshape(4, N_HEADS * HEAD_DIM, d)
        x = _post(o_a, o_b, o_c, o_d, x, mod, wo4, norm_mlp_g[l].reshape(1, d), w_up[l].astype(BF16),
                  w_down[l].astype(BF16), fg, final=(l == depth - 1))
    return x
```

```python
import functools
import math

import numpy as np
import jax
import jax.numpy as jnp
from jax import lax
from jax.experimental import pallas as pl
from jax.experimental.pallas import tpu as pltpu

F32 = jnp.float32
BF16 = jnp.bfloat16

D_MODEL = 1024
DEPTH = 2
HEAD_DIM = 64
LANES = 128
QB = 128
NEG = -1e30
TINY = 1e-30
BIG = 1e9
RMS_EPS = 1e-6
N_ADA = 6
N_HEADS = 4

SWA_KV_HEADS = 2
SWA_WINDOW = 128
MLA_Q_RANK = 192
MLA_KV_RANK = 128
MLA_NOPE = 64
MLA_ROPE = 32
ROPE_THETA = 10000.0
NSA_CMP_LEN = 32
NSA_CMP_STRIDE = 16
NSA_CMP_HIDDEN = 128
NSA_SEL_BLOCK = 64
NSA_SEL_SHIFT = 6
NSA_TOPN = 16
NSA_WINDOW = 512
NSA_N_BRANCH = 3
MOBA_BLOCK = 256
MOBA_SHIFT = 8
MOBA_TOPK = 3
MOBA_LANE0 = 96
D_FF = 4 * D_MODEL
N_ALIBI = 3 * N_HEADS

_S_ALL = [2.0 ** (-8.0 * (i + 1) / N_ALIBI) for i in range(N_ALIBI)]
SLOPES_A = [float(np.float32(v)) for v in _S_ALL[0::3]]
SLOPES_C = [float(np.float32(v)) for v in _S_ALL[1::3]]
SLOPES_D = [float(np.float32(v)) for v in _S_ALL[2::3]]

VMEM_LIMIT = 56 * 1024 * 1024

_NT = (((1,), (1,)), ((), ()))


def _cparams(sem):
    return pltpu.CompilerParams(dimension_semantics=sem, vmem_limit_bytes=VMEM_LIMIT)


def _split3(a):
    hi = a.astype(BF16)
    r1 = a - hi.astype(F32)
    mid = r1.astype(BF16)
    lo = (r1 - mid.astype(F32)).astype(BF16)
    return hi, mid, lo


def _dot(a, b):
    return jnp.dot(a, b, preferred_element_type=F32)


def _dot_nt(a, b):
    return lax.dot_general(a, b, _NT, preferred_element_type=F32)


def _split3_const(x):
    x = np.float32(x)
    hi = np.float32(np.asarray(x, dtype=BF16))
    mid = np.float32(np.asarray(np.float32(x - hi), dtype=BF16))
    lo = np.float32(np.asarray(np.float32(x - hi - mid), dtype=BF16))
    return float(hi), float(mid), float(lo)


def _pieces_lanes(lane, pieces, repeat):
    out = jnp.zeros(lane.shape, F32)
    for i, pc in enumerate(pieces):
        lo = HEAD_DIM + i * repeat
        out = jnp.where(jnp.logical_and(lane >= lo, lane < lo + repeat), pc, out)
    return out


def _pos_lanes(lane, pos):
    hi = ((pos >> 7) << 7).astype(F32)
    lo = (pos & 127).astype(F32)
    k = lane - HEAD_DIM
    return jnp.where(jnp.logical_and(k >= 0, k < 6), jnp.where((k & 1) == 0, hi, lo), 0.0)


def _ada_kernel(c_ref, w_ref, b_ref, o_ref):
    c = c_ref[...]
    a = c * jax.nn.sigmoid(c)
    w = w_ref[0]
    a_hi = a.astype(BF16)
    a_lo = (a - a_hi.astype(F32)).astype(BF16)
    w_hi = w.astype(BF16)
    w_lo = (w - w_hi.astype(F32)).astype(BF16)
    o_ref[0] = _dot(a_hi, w_hi) + _dot(a_hi, w_lo) + _dot(a_lo, w_hi) + b_ref[0]


def _ada(c8, w_ada, b_ada):
    depth, d, n = w_ada.shape
    nb = n // d
    return pl.pallas_call(
        _ada_kernel,
        grid=(depth, nb),
        in_specs=[
            pl.BlockSpec((8, d), lambda l, j: (0, 0)),
            pl.BlockSpec((1, d, d), lambda l, j: (l, 0, j)),
            pl.BlockSpec((1, 1, d), lambda l, j: (l, 0, j)),
        ],
        out_specs=pl.BlockSpec((1, 8, d), lambda l, j: (l, 0, j)),
        out_shape=jax.ShapeDtypeStruct((depth, 8, n), F32),
        compiler_params=_cparams(("parallel", "parallel")),
        name="ada_mod",
    )(c8, w_ada, b_ada.reshape(depth, 1, n))


_IN_GROUPS = (("mla", 640), ("A", 512), ("C", 768), ("D", 768))
_IN_OFFS = {}
_o = 0
for _n, _w in _IN_GROUPS:
    _IN_OFFS[_n] = (_o, _o + _w)
    _o += _w
IN_COLS = _o


def _in_proj_kernel(x_ref, mod_ref, g_ref, w_ref, qg_ref, kvg_ref, wq_ref, wkv_ref, cos_ref, sin_ref,
                    qB_ref, kB_ref, vB_ref, qA_ref, kA_ref, vA_ref, qC_ref, kcvc_ref, kvC_ref, gl_ref,
                    qD_ref, kD_ref, vD_ref):
    x = x_ref[0]
    y = x * lax.rsqrt(jnp.mean(x * x, axis=-1, keepdims=True) + RMS_EPS) * g_ref[...]
    h = y * (1.0 + mod_ref[0, 1:2, :]) + mod_ref[0, 0:1, :]
    hb = h.astype(BF16)

    def proj(name):
        a, b = _IN_OFFS[name]
        return _dot(hb, w_ref[:, a:b])

    ts = x.shape[0]
    lane = lax.broadcasted_iota(jnp.int32, (ts, LANES), 1)
    pos = lax.broadcasted_iota(jnp.int32, (ts, LANES), 0) + pl.program_id(1) * ts
    low = lane < HEAD_DIM
    one64 = jnp.where(lane == HEAD_DIM, 1.0, 0.0)
    ones3 = jnp.where(jnp.logical_and(lane >= HEAD_DIM, lane < HEAD_DIM + 3), 1.0, 0.0)
    blk_hot = jnp.where(lane - MOBA_LANE0 == (pos >> MOBA_SHIFT), 1.0, 0.0)
    posf = pos.astype(F32)
    pos_lanes = _pos_lanes(lane, pos)

    def head_tile(z, hd, extra):
        pair = z[:, (hd // 2) * LANES:(hd // 2 + 1) * LANES]
        if hd % 2:
            pair = pltpu.roll(pair, HEAD_DIM, 1)
        return jnp.where(low, pair, extra).astype(BF16)

    za = proj("A")
    for hd in range(N_HEADS):
        qA_ref[0, :, hd * LANES:(hd + 1) * LANES] = head_tile(
            za, hd, _pieces_lanes(lane, _split3_const(SLOPES_A[hd]), repeat=2))
    for kv in range(SWA_KV_HEADS):
        sl = slice(kv * LANES, (kv + 1) * LANES)
        kA_ref[0, :, sl] = head_tile(za, N_HEADS + kv, pos_lanes)
        vA_ref[0, :, sl] = head_tile(za, N_HEADS + SWA_KV_HEADS + kv, one64)

    zc = proj("C")
    for hd in range(N_HEADS):
        qC_ref[0, :, hd * LANES:(hd + 1) * LANES] = head_tile(
            zc, hd, _pieces_lanes(lane, _split3_const(SLOPES_C[hd]), repeat=2))
    kcvc_ref[0] = zc[:, 2 * LANES:3 * LANES].astype(BF16)
    kvC_ref[0, :, 0:LANES] = head_tile(zc, 6, pos_lanes)
    kvC_ref[0, :, LANES:2 * LANES] = jnp.where(lane == (pos >> NSA_SEL_SHIFT), 1.0, 0.0).astype(BF16)
    kvC_ref[0, :, 2 * LANES:3 * LANES] = head_tile(zc, 7, one64)
    kvC_ref[0, :, 3 * LANES:4 * LANES] = head_tile(zc, 8, pos_lanes)
    kvC_ref[0, :, 4 * LANES:5 * LANES] = head_tile(zc, 9, one64)
    gl_ref[0] = zc[:, 5 * LANES:6 * LANES]

    zd = proj("D")
    for hd in range(N_HEADS):
        sl = slice(hd * LANES, (hd + 1) * LANES)
        hi, mid, lo = _split3(posf * SLOPES_D[hd])
        alibi = jnp.where(lane == HEAD_DIM, hi.astype(F32),
                          jnp.where(lane == HEAD_DIM + 1, mid.astype(F32),
                                    jnp.where(lane == HEAD_DIM + 2, lo.astype(F32), 0.0)))
        qD_ref[0, :, sl] = head_tile(zd, hd, ones3)
        kD_ref[0, :, sl] = head_tile(zd, N_HEADS + hd, alibi + blk_hot)
        vD_ref[0, :, sl] = head_tile(zd, 2 * N_HEADS + hd, one64)

    zb = proj("mla")
    cq = zb[:, 0:256]
    ckv = zb[:, 256:384]
    kpm = zb[:, 384:512]
    kpr = zb[:, 512:640]
    cqn = cq * lax.rsqrt(jnp.sum(cq * cq, axis=-1, keepdims=True) * (1.0 / MLA_Q_RANK) + RMS_EPS) * qg_ref[...]
    ckvn = ckv * lax.rsqrt(jnp.mean(ckv * ckv, axis=-1, keepdims=True) + RMS_EPS) * kvg_ref[...]
    qall = _dot(cqn.astype(BF16), wq_ref[...])
    kvall = _dot(ckvn.astype(BF16), wkv_ref[...])
    cosq = cos_ref[...]
    sinq = sin_ref[...]
    kpe = kpm * cosq + kpr * sinq
    for hd in range(N_HEADS):
        lo = hd * 2 * LANES
        sl = slice(hd * LANES, (hd + 1) * LANES)
        qB_ref[0, :, sl] = (qall[:, lo:lo + LANES] * cosq + qall[:, lo + LANES:lo + 2 * LANES] * sinq).astype(BF16)
        kB_ref[0, :, sl] = (kvall[:, lo:lo + LANES] + kpe).astype(BF16)
        vB_ref[0, :, sl] = (kvall[:, lo + LANES:lo + 2 * LANES] + one64).astype(BF16)


def _in_proj(x, mod, g, wmain, qg, kvg, wq, wkv, cos_t, sin_t, ts=256):
    b, s, d = x.shape
    nt = s // ts
    tok = lambda w: pl.BlockSpec((1, ts, w), lambda i, j: (i, j, 0))
    full = lambda a: pl.BlockSpec(a.shape, lambda i, j: (0,) * a.ndim)
    outs = [("qB", 512, BF16), ("kB", 512, BF16), ("vB", 512, BF16), ("qA", 512, BF16), ("kA", 256, BF16),
            ("vA", 256, BF16), ("qC", 512, BF16), ("kcvc", 128, BF16), ("kvC", 640, BF16), ("gl", 128, F32),
            ("qD", 512, BF16), ("kD", 512, BF16), ("vD", 512, BF16)]
    res = pl.pallas_call(
        _in_proj_kernel,
        grid=(b, nt),
        in_specs=[tok(d), pl.BlockSpec((1, N_ADA, d), lambda i, j: (i, 0, 0)), full(g), full(wmain), full(qg),
                  full(kvg), full(wq), full(wkv),
                  pl.BlockSpec((ts, LANES), lambda i, j: (j, 0)), pl.BlockSpec((ts, LANES), lambda i, j: (j, 0))],
        out_specs=[tok(w) for _, w, _ in outs],
        out_shape=[jax.ShapeDtypeStruct((b, s, w), dt) for _, w, dt in outs],
        compiler_params=_cparams(("parallel", "parallel")),
        name="in_proj",
    )(x, mod, g, wmain, qg, kvg, wq, wkv, cos_t, sin_t)
    return {n: r for (n, _, _), r in zip(outs, res)}


def _fold_max(s):
    out = s[:, 0:LANES]
    for j in range(1, s.shape[1] // LANES):
        out = jnp.maximum(out, s[:, j * LANES:(j + 1) * LANES])
    return out


LOG2E = math.log2(math.e)


def _grouped_loops(n, groups, body, carry):
    pos = 0
    left = n
    for g in groups:
        cnt = left // g
        carry = lax.fori_loop(0, cnt, lambda i, cr, g=g, pos=pos: body(pos + i * g, g, cr), carry)
        pos = pos + cnt * g
        left = left - cnt * g
    return carry


def _two_pass(streams, n_past, tk, s_sc, scale=1.0, groups=(4, 2, 1), chunk_of=None):
    sc2 = scale * LOG2E

    def row0(c):
        return pl.multiple_of(c * tk, tk)

    def pass1(pos, g, mxs):
        out = []
        for i, st in enumerate(streams):
            if chunk_of is None:
                parts_src = [_dot_nt(st["q"], st["k_rows"](row0(pos), g * tk)) * sc2]
                parts = [parts_src[0][:, j * tk:(j + 1) * tk] for j in range(g)]
            else:
                parts = [_dot_nt(st["q"], st["k_rows"](row0(chunk_of(pos + j)), tk)) * sc2 for j in range(g)]
            mx = mxs[i]
            for j in range(g):
                s_sc[i, pos + j] = parts[j]
                mx = jnp.maximum(mx, _fold_max(parts[j]))
            out.append(mx)
        return tuple(out)

    own = [st["s_own"] * LOG2E for st in streams]
    mxs = _grouped_loops(n_past, groups, pass1, tuple(_fold_max(s) for s in own))
    m128 = [jnp.broadcast_to(jnp.max(mx, axis=1, keepdims=True), mx.shape) for mx in mxs]

    def p_of(s, mt):
        return jnp.exp2(s - jnp.concatenate([mt] * (s.shape[1] // LANES), axis=1)).astype(BF16)

    def pass2(pos, g, accs):
        out = []
        for i, st in enumerate(streams):
            if chunk_of is None:
                p = jnp.concatenate([p_of(s_sc[i, pos + j], m128[i]) for j in range(g)], axis=1)
                out.append(accs[i] + _dot(p, st["v_rows"](row0(pos), g * tk)))
            else:
                acc = accs[i]
                for j in range(g):
                    acc = acc + _dot(p_of(s_sc[i, pos + j], m128[i]), st["v_rows"](row0(chunk_of(pos + j)), tk))
                out.append(acc)
        return tuple(out)

    init = tuple(_dot(p_of(own[i], m128[i]), st["v_own"]) for i, st in enumerate(streams))
    return _grouped_loops(n_past, groups, pass2, init)


def _finish_ones(acc):
    return acc[:, :HEAD_DIM] * (1.0 / jnp.maximum(acc[:, HEAD_DIM:HEAD_DIM + 1], TINY))


def _dist(tq, tk, t0, k0):
    r = lax.broadcasted_iota(jnp.int32, (tq, tk), 0)
    c = lax.broadcasted_iota(jnp.int32, (tq, tk), 1)
    return (r - c) + (t0 - k0)


SWA_TQ = 256


def _swa_kernel(sink_ref, q_ref, k_ref, v_ref, o_ref):
    t0 = pl.program_id(1) * SWA_TQ
    slab = SWA_WINDOW + SWA_TQ
    k0 = pl.multiple_of(jnp.maximum(t0 - SWA_WINDOW, 0), SWA_WINDOW)
    dist = _dist(SWA_TQ, slab, t0, k0)
    ok = jnp.logical_and(dist >= 0, dist < SWA_WINDOW)
    tpos = (lax.broadcasted_iota(jnp.int32, (SWA_TQ, 1), 0) + t0).astype(F32)
    outs = []
    for hd in range(N_HEADS):
        kv = hd // (N_HEADS // SWA_KV_HEADS)
        ksl = slice(kv * LANES, (kv + 1) * LANES)
        s = jnp.where(ok, _dot_nt(q_ref[0, :, hd * LANES:(hd + 1) * LANES], k_ref[0, pl.ds(k0, slab), ksl]), NEG)
        sink = sink_ref[hd] + SLOPES_A[hd] * tpos
        m = jnp.maximum(jnp.max(s, axis=1, keepdims=True), sink)
        acc = _dot(jnp.exp(s - m).astype(BF16), v_ref[0, pl.ds(k0, slab), ksl])
        den = acc[:, HEAD_DIM:HEAD_DIM + 1] + jnp.exp(sink - m)
        outs.append(acc[:, :HEAD_DIM] * (1.0 / jnp.maximum(den, TINY)))
    o_ref[0] = jnp.concatenate(outs, axis=1).astype(o_ref.dtype)


def _swa(qA, kA, vA, sinks):
    b, s, _ = qA.shape
    assert s % SWA_TQ == 0 and s >= SWA_WINDOW + SWA_TQ
    tile = lambda w: pl.BlockSpec((1, SWA_TQ, w), lambda i, j: (i, j, 0))
    whole = lambda w: pl.BlockSpec((1, s, w), lambda i, j: (i, 0, 0))
    return pl.pallas_call(
        _swa_kernel,
        grid=(b, s // SWA_TQ),
        in_specs=[pl.BlockSpec(memory_space=pltpu.SMEM), tile(N_HEADS * LANES), whole(SWA_KV_HEADS * LANES),
                  whole(SWA_KV_HEADS * LANES)],
        out_specs=tile(N_HEADS * HEAD_DIM),
        out_shape=jax.ShapeDtypeStruct((b, s, N_HEADS * HEAD_DIM), BF16),
        compiler_params=_cparams(("parallel", "arbitrary")),
        name="swa",
    )(sinks, qA, kA, vA)


MLA_SCALE = float((MLA_NOPE + MLA_ROPE) ** -0.5)


def _rows_at(ref, sl):
    return lambda row0, n: ref[0, pl.ds(row0, n), sl]


def _mla_kernel(q_ref, k_ref, v_ref, o_ref, s_sc, *, tq, tk):
    t0 = pl.program_id(2) * tq
    n_past = t0 // tk
    k_own = pl.multiple_of(n_past * tk, tk)
    bias = jnp.where(_dist(tq, tk, t0, k_own) >= 0, 0.0, NEG)
    streams = []
    for hd in range(2):
        sl = slice(hd * LANES, (hd + 1) * LANES)
        q = q_ref[0, :, sl]
        s_own = _dot_nt(q, k_ref[0, pl.ds(k_own, tk), sl]) * MLA_SCALE + bias
        streams.append(dict(q=q, k_rows=_rows_at(k_ref, sl), v_rows=_rows_at(v_ref, sl), s_own=s_own,
                            v_own=v_ref[0, pl.ds(k_own, tk), sl]))
    accs = _two_pass(streams, n_past, tk, s_sc, scale=MLA_SCALE)
    o_ref[0] = jnp.concatenate([_finish_ones(acc) for acc in accs], axis=1).astype(o_ref.dtype)


def _mla(qB, kB, vB, tq=512, tk=512):
    b, s, _ = qB.shape
    tk = min(tk, s)
    nq = s // tq
    resident = lambda: pl.BlockSpec((1, s, 2 * LANES), lambda bi, p, i: (bi, 0, p), pipeline_mode=pl.Buffered(1))
    return pl.pallas_call(
        functools.partial(_mla_kernel, tq=tq, tk=tk),
        grid=(b, 2, nq),
        in_specs=[pl.BlockSpec((1, tq, 2 * LANES), lambda bi, p, i: (bi, i, p)), resident(), resident()],
        out_specs=pl.BlockSpec((1, tq, LANES), lambda bi, p, i: (bi, i, p)),
        out_shape=jax.ShapeDtypeStruct((b, s, N_HEADS * HEAD_DIM), BF16),
        scratch_shapes=[pltpu.VMEM((2, max(s // tk - 1, 1), tq, tk), F32)],
        compiler_params=_cparams(("parallel", "parallel", "arbitrary")),
        name="mla_attn",
    )(qB, kB, vB)


def _gelu_tanh(x):
    return x * (0.5 * (1.0 + jnp.tanh(math.sqrt(2.0 / math.pi) * (x + 0.044715 * (x * x * x)))))


def _nsa_cmp_kernel(ch_ref, w1_ref, pos_ref, w1f_ref, w2_ref, kc_ref, vc_ref):
    nc = ch_ref.shape[1]
    u = _dot(ch_ref[0], w1_ref[...])
    for idx, out in ((0, kc_ref), (1, vc_ref)):
        pos = jnp.broadcast_to(pos_ref[idx], (8, pos_ref.shape[2]))
        w1f = w1f_ref[idx]
        p_hi = pos.astype(BF16)
        p_lo = (pos - p_hi.astype(F32)).astype(BF16)
        w_hi = w1f.astype(BF16)
        w_lo = (w1f - w_hi.astype(F32)).astype(BF16)
        posb = (_dot(p_hi, w_hi) + _dot(p_hi, w_lo) + _dot(p_lo, w_hi))[0:1, :]
        top = u[:, (2 * idx) * LANES:(2 * idx + 1) * LANES]
        bot = u[:, (2 * idx + 1) * LANES:(2 * idx + 2) * LANES]
        hid = top + pltpu.roll(bot, nc - 1, 0) + posb
        res = _dot(_gelu_tanh(hid).astype(BF16), w2_ref[idx])
        if idx == 0:
            lane = lax.broadcasted_iota(jnp.int32, (nc, LANES), 1)
            cend = lax.broadcasted_iota(jnp.int32, (nc, LANES), 0) * NSA_CMP_STRIDE + (NSA_CMP_LEN - 1)
            res = res + _pos_lanes(lane, cend)
        out[0] = res.astype(out.dtype)


def _nsa_compress(ch, w1aug, pos2, w1f, w2p):
    b, nc, w = ch.shape
    full = lambda a: pl.BlockSpec(a.shape, lambda i: (0,) * a.ndim)
    return pl.pallas_call(
        _nsa_cmp_kernel,
        grid=(b,),
        in_specs=[pl.BlockSpec((1, nc, w), lambda i: (i, 0, 0)), full(w1aug), full(pos2), full(w1f), full(w2p)],
        out_specs=[pl.BlockSpec((1, nc, LANES), lambda i: (i, 0, 0))] * 2,
        out_shape=[jax.ShapeDtypeStruct((b, nc, LANES), BF16)] * 2,
        compiler_params=_cparams(("parallel",)),
        name="nsa_compress",
    )(ch, w1aug, pos2, w1f, w2p)


NSA_CHUNK = 256


def _nsa_kernel(q_ref, kc_ref, vc_ref, kv_ref, gl_ref, mm_ref, o_ref, imp_sc, s_sc, act_sm):
    n = pl.program_id(1)
    t0 = n * QB
    nc = kc_ref.shape[1]
    nbl = LANES
    qh = [q_ref[0, :, hd * LANES:(hd + 1) * LANES] for hd in range(N_HEADS)]

    def stack(x):
        return jnp.concatenate([x] * N_HEADS, axis=0)

    qs = jnp.concatenate(qh, axis=0)

    slab = NSA_WINDOW + QB
    k0w = pl.multiple_of(jnp.maximum(n - NSA_WINDOW // QB, 0) * QB, QB)
    dist = _dist(QB, slab, t0, k0w)
    bias_w = stack(jnp.where(jnp.logical_and(dist >= 0, dist < NSA_WINDOW), 0.0, NEG))
    s = _dot_nt(qs, kv_ref[0, pl.ds(k0w, slab), 3 * LANES:4 * LANES]) + bias_w
    pw = jnp.exp(s - jnp.max(s, axis=1, keepdims=True))
    o_win = _finish_ones(_dot(pw.astype(BF16), kv_ref[0, pl.ds(k0w, slab), 4 * LANES:5 * LANES]))

    tq1 = lax.broadcasted_iota(jnp.int32, (QB, 1), 0) + t0
    cidx = lax.broadcasted_iota(jnp.int32, (QB, nc), 1)
    ok = jnp.logical_and(tq1 >= cidx * NSA_CMP_STRIDE + (NSA_CMP_LEN - 1), cidx < nc - 1)
    s = _dot_nt(qs, kc_ref[0]) + stack(jnp.where(ok, 0.0, NEG))
    e = jnp.exp(s - jnp.max(s, axis=1, keepdims=True))
    row_ok = stack(jnp.where(tq1 >= NSA_CMP_LEN - 1, 1.0, 0.0))
    p = e * (row_ok / jnp.maximum(jnp.sum(e, axis=1, keepdims=True), TINY))
    o_cmp = _dot(p.astype(BF16), vc_ref[0])
    psum = p[0:QB]
    for hd in range(1, N_HEADS):
        psum = psum + p[hd * QB:(hd + 1) * QB]

    mm = mm_ref[...]
    imp = None
    for piece in _split3(psum):
        d = _dot(piece, mm)
        imp = d if imp is None else imp + d
    imp_t = imp.T
    jrow = lax.broadcasted_iota(jnp.int32, (nbl, QB), 0)
    tcol = lax.broadcasted_iota(jnp.int32, (nbl, QB), 1) + t0
    cur = tcol >> 6
    cand = jrow <= cur
    forced = jnp.logical_or(jnp.logical_or(jrow == 0, jrow == cur), jrow == cur - 1)
    a = jnp.where(cand, jnp.where(forced, BIG, imp_t), NEG)
    key = lax.bitcast_convert_type(a, jnp.int32)
    key1 = key + 1
    imp_sc[...] = key
    n_cand = (t0 + QB - 1) // NSA_SEL_BLOCK + 1

    def rank_body(i2, cnt):
        for u in range(2):
            i = 2 * i2 + u
            row = imp_sc[pl.ds(i, 1), :]
            cnt = cnt + jnp.where(row >= jnp.where(jrow <= i, key1, key), 1.0, 0.0)
        return cnt

    rank = lax.fori_loop(0, n_cand // 2, rank_body, jnp.zeros((nbl, QB), F32))
    sel_t = jnp.where(jnp.logical_and(rank < NSA_TOPN, cand), 0.0, NEG)
    selneg = sel_t.T.astype(BF16)

    q_sel = jnp.concatenate([jnp.concatenate([qh[hd], selneg], axis=1) for hd in range(N_HEADS)], axis=0)
    own = t0 // NSA_CHUNK
    k_own = pl.multiple_of(own * NSA_CHUNK, NSA_CHUNK)
    ksl, vsl = slice(0, 2 * LANES), slice(2 * LANES, 3 * LANES)
    s_own = (_dot_nt(q_sel, kv_ref[0, pl.ds(k_own, NSA_CHUNK), ksl])
             + stack(jnp.where(_dist(QB, NSA_CHUNK, t0, k_own) >= 0, 0.0, NEG)))
    n_chunks = kv_ref.shape[1] // NSA_CHUNK
    blk_shift = (NSA_CHUNK // NSA_SEL_BLOCK).bit_length() - 1
    grp = jnp.where((lax.broadcasted_iota(jnp.int32, (n_chunks, nbl), 1) >> blk_shift)
                    == lax.broadcasted_iota(jnp.int32, (n_chunks, nbl), 0), 1.0, 0.0).astype(BF16)
    picked = jnp.where(sel_t == 0.0, 1.0, 0.0).astype(BF16)
    cnt = _dot(_dot(grp, picked).astype(BF16), jnp.ones((QB, LANES), BF16))
    n_act = jnp.int32(0)
    for c in range(n_chunks):
        act_sm[n_act] = c
        n_act = n_act + jnp.where(jnp.logical_and(cnt[c, 0] > 0.0, c < own), 1, 0)
    stream = dict(q=q_sel, k_rows=_rows_at(kv_ref, ksl), v_rows=_rows_at(kv_ref, vsl),
                  s_own=s_own, v_own=kv_ref[0, pl.ds(k_own, NSA_CHUNK), vsl])
    o_sel = _finish_ones(_two_pass([stream], n_act, NSA_CHUNK, s_sc, groups=(4, 2, 1),
                                   chunk_of=lambda pos: act_sm[pos])[0])

    g = jax.nn.sigmoid(gl_ref[0])
    outs = []
    for hd in range(N_HEADS):
        rs = slice(hd * QB, (hd + 1) * QB)
        outs.append(g[:, hd:hd + 1] * o_cmp[rs, :HEAD_DIM] + g[:, N_HEADS + hd:N_HEADS + hd + 1] * o_sel[rs]
                    + g[:, 2 * N_HEADS + hd:2 * N_HEADS + hd + 1] * o_win[rs])
    o_ref[0] = jnp.concatenate(outs, axis=1).astype(o_ref.dtype)


def _nsa(qC, kcmp, vcmp, kvC, gl, m_mat):
    b, s, _ = qC.shape
    nq = s // QB
    nc = kcmp.shape[1]
    assert s >= NSA_WINDOW + QB and s % NSA_CHUNK == 0
    return pl.pallas_call(
        _nsa_kernel,
        grid=(b, nq),
        in_specs=[pl.BlockSpec((1, QB, N_HEADS * LANES), lambda i, j: (i, j, 0)),
                  pl.BlockSpec((1, nc, LANES), lambda i, j: (i, 0, 0)),
                  pl.BlockSpec((1, nc, LANES), lambda i, j: (i, 0, 0)),
                  pl.BlockSpec((1, s, 5 * LANES), lambda i, j: (i, 0, 0)),
                  pl.BlockSpec((1, QB, LANES), lambda i, j: (i, j, 0)),
                  pl.BlockSpec(m_mat.shape, lambda i, j: (0, 0))],
        out_specs=pl.BlockSpec((1, QB, N_HEADS * HEAD_DIM), lambda i, j: (i, j, 0)),
        out_shape=jax.ShapeDtypeStruct((b, s, N_HEADS * HEAD_DIM), BF16),
        scratch_shapes=[pltpu.VMEM((LANES, QB), jnp.int32),
                        pltpu.VMEM((1, max(s // NSA_CHUNK - 1, 1), N_HEADS * QB, NSA_CHUNK), F32),
                        pltpu.SMEM((s // NSA_CHUNK + 1,), jnp.int32)],
        compiler_params=_cparams(("parallel", "arbitrary")),
        name="nsa_attn",
    )(qC, kcmp, vcmp, kvC, gl, m_mat)


MOBA_NBR = LANES - MOBA_LANE0


MOBA_TQ = 2 * MOBA_BLOCK


def _moba_kernel(q_ref, k_ref, v_ref, o_ref, kmean_sc, gate_sc, s_sc):
    tq = MOBA_TQ
    n = pl.program_id(2)
    t0 = n * tq
    n_blk = k_ref.shape[1] // MOBA_BLOCK
    lane = lax.broadcasted_iota(jnp.int32, (1, LANES), 1)

    @pl.when(n == 0)
    def _():
        kmean_sc[...] = jnp.zeros(kmean_sc.shape, F32)
        for hd in range(2):
            for jb in range(n_blk):
                blk = k_ref[0, jb * MOBA_BLOCK:(jb + 1) * MOBA_BLOCK, hd * LANES:(hd + 1) * LANES].astype(F32)
                kmean_sc[hd, jb:jb + 1, :] = jnp.where(lane < HEAD_DIM, jnp.mean(blk, axis=0, keepdims=True), 0.0)

    jrow = lax.broadcasted_iota(jnp.int32, (MOBA_NBR, tq), 0)
    qcol = lax.broadcasted_iota(jnp.int32, (MOBA_NBR, tq), 1)
    cur = 2 * n + (qcol >> MOBA_SHIFT)
    qs = [q_ref[0, :, hd * LANES:(hd + 1) * LANES] for hd in range(2)]
    gates = []
    for hd in range(2):
        gate_t = _dot_nt(kmean_sc[hd].astype(BF16), qs[hd])[0:MOBA_NBR, :]
        a = jnp.where(jrow < cur, gate_t, NEG)
        gate_sc[hd] = a
        gates.append(a)

    def rank_body(i, cnts):
        out = []
        for hd in range(2):
            row = gate_sc[hd, pl.ds(i, 1), :]
            c_ge = jnp.where(row >= gates[hd], 1.0, 0.0)
            c_gt = jnp.where(row > gates[hd], 1.0, 0.0)
            out.append(cnts[hd] + jnp.where(jrow > i, c_ge, c_gt))
        return tuple(out)

    ranks = lax.fori_loop(0, 2 * n + 1, rank_body, tuple(jnp.zeros((MOBA_NBR, tq), F32) for _ in range(2)))
    tk = MOBA_TQ
    k_own = pl.multiple_of(t0, tk)
    bias = jnp.where(_dist(tq, tk, 0, 0) >= 0, 0.0, NEG)
    streams = []
    for hd in range(2):
        sl = slice(hd * LANES, (hd + 1) * LANES)
        keep = jnp.logical_or(jnp.logical_and(ranks[hd] < MOBA_TOPK, jrow < cur), jrow == cur)
        full_t = jnp.concatenate([jnp.zeros((MOBA_LANE0, tq), F32), jnp.where(keep, 0.0, NEG)], axis=0)
        q_sel = (qs[hd].astype(F32) + full_t.T).astype(BF16)
        s_own = _dot_nt(q_sel, k_ref[0, pl.ds(k_own, tk), sl]) + bias
        streams.append(dict(q=q_sel, k_rows=_rows_at(k_ref, sl), v_rows=_rows_at(v_ref, sl), s_own=s_own,
                            v_own=v_ref[0, pl.ds(k_own, tk), sl]))
    accs = _two_pass(streams, n, tk, s_sc)
    o_ref[0] = jnp.concatenate([_finish_ones(acc) for acc in accs], axis=1).astype(o_ref.dtype)


def _moba(qD, kD, vD):
    b, s, _ = qD.shape
    tq = MOBA_TQ
    assert s % tq == 0 and s // MOBA_BLOCK <= MOBA_NBR
    resident = lambda: pl.BlockSpec((1, s, 2 * LANES), lambda i, p, j: (i, 0, p), pipeline_mode=pl.Buffered(1))
    return pl.pallas_call(
        _moba_kernel,
        grid=(b, 2, s // tq),
        in_specs=[pl.BlockSpec((1, tq, 2 * LANES), lambda i, p, j: (i, j, p)), resident(), resident()],
        out_specs=pl.BlockSpec((1, tq, LANES), lambda i, p, j: (i, j, p)),
        out_shape=jax.ShapeDtypeStruct((b, s, N_HEADS * HEAD_DIM), BF16),
        scratch_shapes=[pltpu.VMEM((2, LANES, LANES), F32), pltpu.VMEM((2, MOBA_NBR, tq), F32),
                        pltpu.VMEM((2, max(s // tq - 1, 1), tq, tq), F32)],
        compiler_params=_cparams(("parallel", "parallel", "arbitrary")),
        name="moba_attn",
    )(qD, kD, vD)


def _post_kernel(oa_ref, ob_ref, oc_ref, od_ref, x_ref, mod_ref, wo_ref, g_ref, wu_ref, wd_ref, fg_ref, out_ref,
                 *, final, ffc):
    x = x_ref[0]
    mix = None
    for idx, ref in enumerate((oa_ref, ob_ref, oc_ref, od_ref)):
        d = _dot(ref[0], wo_ref[idx])
        mix = d if mix is None else mix + d
    x1 = x + mod_ref[0, 2:3, :] * mix
    y = x1 * lax.rsqrt(jnp.mean(x1 * x1, axis=-1, keepdims=True) + RMS_EPS) * g_ref[...]
    hb = (y * (1.0 + mod_ref[0, 4:5, :]) + mod_ref[0, 3:4, :]).astype(BF16)
    acc = None
    for c in range(wu_ref.shape[1] // ffc):
        hid = jnp.maximum(_dot(hb, wu_ref[:, c * ffc:(c + 1) * ffc]), 0.0)
        d = _dot((hid * hid).astype(BF16), wd_ref[c * ffc:(c + 1) * ffc, :])
        acc = d if acc is None else acc + d
    x2 = x1 + mod_ref[0, 5:6, :] * acc
    if final:
        x2 = x2 * lax.rsqrt(jnp.mean(x2 * x2, axis=-1, keepdims=True) + RMS_EPS) * fg_ref[...]
    out_ref[0] = x2


def _post(oa, ob, oc, od, x, mod, wo4, g, wu, wd, fg, final, ts=256, ffc=1024):
    b, s, d = x.shape
    nt = s // ts
    tok = lambda w: pl.BlockSpec((1, ts, w), lambda i, j: (i, j, 0))
    full = lambda a: pl.BlockSpec(a.shape, lambda i, j: (0,) * a.ndim, pipeline_mode=pl.Buffered(1))
    hw = N_HEADS * HEAD_DIM
    return pl.pallas_call(
        functools.partial(_post_kernel, final=final, ffc=ffc),
        grid=(b, nt),
        in_specs=[tok(hw), tok(hw), tok(hw), tok(hw), tok(d), pl.BlockSpec((1, N_ADA, d), lambda i, j: (i, 0, 0)),
                  full(wo4), full(g), full(wu), full(wd), full(fg)],
        out_specs=tok(d),
        out_shape=jax.ShapeDtypeStruct((b, s, d), F32),
        compiler_params=_cparams(("parallel", "parallel")),
        name="post_mlp",
    )(oa, ob, oc, od, x, mod, wo4, g, wu, wd, fg)


def _rot_half_cols(w):
    half = w.shape[1] // 2
    return jnp.concatenate([-w[:, half:], w[:, :half]], axis=1)


def _layer_weights(w_in, mla_qg, mla_kvg, w_uq, w_ukv, pos_k, pos_v, ck_w1, ck_w2, cv_w1, cv_w2):
    d = w_in.shape[0]
    z = lambda n: jnp.zeros((d, n), F32)
    o = 0

    def take(n):
        nonlocal o
        r = w_in[:, o:o + n]
        o += n
        return r

    qa, ka, va = take(256), take(128), take(128)
    cq, ckv, kpe = take(MLA_Q_RANK), take(MLA_KV_RANK), take(MLA_ROPE)
    qc, kc, vc, ks, vs, kw, vw, gl = take(256), take(64), take(64), take(64), take(64), take(64), take(64), take(12)
    qd, kd, vd = take(256), take(256), take(256)
    sc = 1.0 / 8.0
    kpe_main = jnp.concatenate([z(64), kpe, z(32)], axis=1)
    kpe_rot = jnp.concatenate([z(64), _rot_half_cols(kpe), z(32)], axis=1)
    p64 = lambda w: jnp.pad(w, ((0, 0), (0, LANES - w.shape[1])))
    groups = {
        "mla": jnp.concatenate([cq, z(64), ckv, kpe_main, kpe_rot], axis=1),
        "A": jnp.concatenate([qa * sc, ka, va], axis=1),
        "C": jnp.concatenate([qc * sc, kc, vc, ks, vs, kw, vw, p64(gl)], axis=1),
        "D": jnp.concatenate([qd * sc, kd, vd], axis=1),
    }
    wmain = jnp.concatenate([groups[n] for n, _ in _IN_GROUPS], axis=1).astype(BF16)

    dq = MLA_NOPE + MLA_ROPE
    wq_cols, wkv_cols = [], []
    for hd in range(N_HEADS):
        wh = w_uq[:, hd * dq:(hd + 1) * dq]
        zq = lambda n: jnp.zeros((MLA_Q_RANK, n), F32)
        wq_cols += [wh, zq(32), zq(64), _rot_half_cols(wh[:, MLA_NOPE:]), zq(32)]
        wk = w_ukv[:, hd * 128:(hd + 1) * 128]
        zk = jnp.zeros((MLA_KV_RANK, 64), F32)
        wkv_cols += [wk[:, :64], zk, wk[:, 64:], zk]
    wq = jnp.pad(jnp.concatenate(wq_cols, axis=1), ((0, 256 - MLA_Q_RANK), (0, 0))).astype(BF16)
    wkv = jnp.concatenate(wkv_cols, axis=1).astype(BF16)
    qg = jnp.pad(mla_qg, (0, 256 - MLA_Q_RANK)).reshape(1, 256)
    kvg = mla_kvg.reshape(1, MLA_KV_RANK)


    def aug(w1, is_v):
        w = w1.reshape(NSA_CMP_LEN, HEAD_DIM, NSA_CMP_HIDDEN)
        zz = jnp.zeros_like(w)
        w = jnp.concatenate([zz, w] if is_v else [w, zz], axis=1)
        w = w.reshape(NSA_CMP_LEN * LANES, NSA_CMP_HIDDEN)
        return w[:NSA_CMP_STRIDE * LANES], w[NSA_CMP_STRIDE * LANES:]

    kt, kb = aug(ck_w1, False)
    vt, vb = aug(cv_w1, True)
    w1aug = jnp.concatenate([kt, kb, vt, vb], axis=1).astype(BF16)
    pos2 = jnp.stack([pos_k.reshape(1, -1), pos_v.reshape(1, -1)])
    w1f = jnp.stack([ck_w1, cv_w1])
    w2p = jnp.stack([p64(ck_w2), p64(cv_w2)]).astype(BF16)
    return wmain, qg, kvg, wq, wkv, w1aug, pos2, w1f, w2p


def _rope_tables(s):
    half = MLA_ROPE // 2
    freqs = ROPE_THETA ** (-jnp.arange(half, dtype=F32) / half)
    ang = jnp.arange(s, dtype=F32)[:, None] * freqs[None, :]
    cos, sin = jnp.cos(ang), jnp.sin(ang)
    ones, zeros = jnp.ones((s, MLA_NOPE), F32), jnp.zeros((s, MLA_NOPE), F32)
    tail = jnp.zeros((s, LANES - MLA_NOPE - MLA_ROPE), F32)
    return (jnp.concatenate([ones, cos, cos, tail], axis=1), jnp.concatenate([zeros, sin, sin, tail], axis=1))


def _nsa_imp_matrix(s):
    n_sel = s // NSA_SEL_BLOCK
    nc = s // NSA_CMP_STRIDE
    assert n_sel <= LANES
    c = np.arange(nc)[:, None]
    j = np.arange(LANES)[None, :]
    m = ((c >= 4 * j - 1) & (c <= 4 * j + 3) & (c < nc - 1) & (j < n_sel)).astype(np.float32)
    return jnp.asarray(m, BF16)


def kernel(x, c, norm_mix_g, norm_mlp_g, w_ada, b_ada, w_in, w_out, swa_sinks, mla_q_norm_g, mla_kv_norm_g,
           mla_w_uq, mla_w_ukv, nsa_cmp_pos_k, nsa_cmp_pos_v, nsa_cmp_k_w1, nsa_cmp_k_w2, nsa_cmp_v_w1,
           nsa_cmp_v_w2, w_up, w_down, final_norm_g):
    b, s, d = x.shape
    depth = w_in.shape[0]
    c8 = jnp.pad(c, ((0, 8 - b), (0, 0)))
    mod_all = _ada(c8, w_ada, b_ada)
    cos_t, sin_t = _rope_tables(s)
    m_mat = _nsa_imp_matrix(s)
    fg = final_norm_g.reshape(1, d)
    for l in range(depth):
        mod = mod_all[l, :b].reshape(b, N_ADA, d)
        wmain, qg, kvg, wq, wkv, w1aug, pos2, w1f, w2p = _layer_weights(
            w_in[l], mla_q_norm_g[l], mla_kv_norm_g[l], mla_w_uq[l], mla_w_ukv[l], nsa_cmp_pos_k[l],
            nsa_cmp_pos_v[l], nsa_cmp_k_w1[l], nsa_cmp_k_w2[l], nsa_cmp_v_w1[l], nsa_cmp_v_w2[l])
        z = _in_proj(x, mod, norm_mix_g[l].reshape(1, d), wmain, qg, kvg, wq, wkv, cos_t, sin_t)
        o_a = _swa(z["qA"], z["kA"], z["vA"], swa_sinks[l])
        o_b = _mla(z["qB"], z["kB"], z["vB"])
        ch = z["kcvc"].reshape(b, s // NSA_CMP_STRIDE, NSA_CMP_STRIDE * LANES)
        kcmp, vcmp = _nsa_compress(ch, w1aug, pos2, w1f, w2p)
        o_c = _nsa(z["qC"], kcmp, vcmp, z["kvC"], z["gl"], m_mat)
        o_d = _moba(z["qD"], z["kD"], z["vD"])
        wo4 = w_out[l].astype(BF16).reshape(4, N_HEADS * HEAD_DIM, d)
        x = _post(o_a, o_b, o_c, o_d, x, mod, wo4, norm_mlp_g[l].reshape(1, d), w_up[l].astype(BF16),
                  w_down[l].astype(BF16), fg, final=(l == depth - 1))
    return x
```

```python
import functools
import math

import numpy as np
import jax
import jax.numpy as jnp
from jax import lax
from jax.experimental import pallas as pl
from jax.experimental.pallas import tpu as pltpu

F32 = jnp.float32
BF16 = jnp.bfloat16

HEAD_DIM = 64
LANES = 128
SUBLANES = 8
QB = 128
NEG = -1e30
TINY = 1e-30
BIG = 1e9
RMS_EPS = 1e-6
N_ADA = 6
N_HEADS = 4

SWA_KV_HEADS = 2
SWA_WINDOW = 128
MLA_Q_RANK = 192
MLA_KV_RANK = 128
MLA_NOPE = 64
MLA_ROPE = 32
ROPE_THETA = 10000.0
NSA_CMP_LEN = 32
NSA_CMP_STRIDE = 16
NSA_CMP_HIDDEN = 128
NSA_SEL_BLOCK = 64
NSA_SEL_SHIFT = 6
NSA_TOPN = 16
NSA_WINDOW = 512
MOBA_BLOCK = 256
MOBA_SHIFT = 8
MOBA_TOPK = 3
MOBA_LANE0 = 96
N_ALIBI = 3 * N_HEADS

_S_ALL = [2.0 ** (-8.0 * (i + 1) / N_ALIBI) for i in range(N_ALIBI)]
SLOPES_A = [float(np.float32(v)) for v in _S_ALL[0::3]]
SLOPES_C = [float(np.float32(v)) for v in _S_ALL[1::3]]
SLOPES_D = [float(np.float32(v)) for v in _S_ALL[2::3]]

VMEM_LIMIT = 56 * 1024 * 1024

_NT = (((1,), (1,)), ((), ()))


def _cparams(sem):
    return pltpu.CompilerParams(dimension_semantics=sem, vmem_limit_bytes=VMEM_LIMIT)


def _split3(a):
    hi = a.astype(BF16)
    r1 = a - hi.astype(F32)
    mid = r1.astype(BF16)
    lo = (r1 - mid.astype(F32)).astype(BF16)
    return hi, mid, lo


def _dot(a, b):
    return jnp.dot(a, b, preferred_element_type=F32)


def _dot_nt(a, b):
    return lax.dot_general(a, b, _NT, preferred_element_type=F32)


def _split3_const(x):
    x = np.float32(x)
    hi = np.float32(np.asarray(x, dtype=BF16))
    mid = np.float32(np.asarray(np.float32(x - hi), dtype=BF16))
    lo = np.float32(np.asarray(np.float32(x - hi - mid), dtype=BF16))
    return float(hi), float(mid), float(lo)


def _pieces_lanes(lane, pieces, repeat):
    out = jnp.zeros(lane.shape, F32)
    for i, pc in enumerate(pieces):
        lo = HEAD_DIM + i * repeat
        out = jnp.where(jnp.logical_and(lane >= lo, lane < lo + repeat), pc, out)
    return out


def _pos_lanes(lane, pos):
    hi = ((pos >> 7) << 7).astype(F32)
    lo = (pos & 127).astype(F32)
    k = lane - HEAD_DIM
    return jnp.where(jnp.logical_and(k >= 0, k < 6), jnp.where((k & 1) == 0, hi, lo), 0.0)


def _ada_kernel(c_ref, w_ref, b_ref, o_ref):
    c = c_ref[...]
    a = c * jax.nn.sigmoid(c)
    w = w_ref[0]
    a_hi = a.astype(BF16)
    a_lo = (a - a_hi.astype(F32)).astype(BF16)
    w_hi = w.astype(BF16)
    w_lo = (w - w_hi.astype(F32)).astype(BF16)
    o_ref[0] = _dot(a_hi, w_hi) + _dot(a_hi, w_lo) + _dot(a_lo, w_hi) + b_ref[0]


def _ada(c8, w_ada, b_ada):
    depth, d, n = w_ada.shape
    nb = n // d
    return pl.pallas_call(
        _ada_kernel,
        grid=(depth, nb),
        in_specs=[
            pl.BlockSpec((SUBLANES, d), lambda l, j: (0, 0)),
            pl.BlockSpec((1, d, d), lambda l, j: (l, 0, j)),
            pl.BlockSpec((1, 1, d), lambda l, j: (l, 0, j)),
        ],
        out_specs=pl.BlockSpec((1, SUBLANES, d), lambda l, j: (l, 0, j)),
        out_shape=jax.ShapeDtypeStruct((depth, SUBLANES, n), F32),
        compiler_params=_cparams(("parallel", "parallel")),
        name="ada_mod",
    )(c8, w_ada, b_ada.reshape(depth, 1, n))


_IN_GROUPS = (("mla", 640), ("A", 512), ("C", 768), ("D", 768))
_IN_OFFS = {}
_o = 0
for _n, _w in _IN_GROUPS:
    _IN_OFFS[_n] = (_o, _o + _w)
    _o += _w


def _in_proj_kernel(x_ref, mod_ref, g_ref, w_ref, qg_ref, kvg_ref, wq_ref, wkv_ref, cos_ref, sin_ref,
                    qB_ref, kB_ref, vB_ref, qA_ref, kA_ref, vA_ref, qC_ref, kcvc_ref, kvC_ref, gl_ref,
                    qD_ref, kD_ref, vD_ref):
    x = x_ref[0]
    y = x * lax.rsqrt(jnp.mean(x * x, axis=-1, keepdims=True) + RMS_EPS) * g_ref[...]
    h = y * (1.0 + mod_ref[0, 1:2, :]) + mod_ref[0, 0:1, :]
    hb = h.astype(BF16)

    def proj(name):
        a, b = _IN_OFFS[name]
        return _dot(hb, w_ref[:, a:b])

    ts = x.shape[0]
    lane = lax.broadcasted_iota(jnp.int32, (ts, LANES), 1)
    pos = lax.broadcasted_iota(jnp.int32, (ts, LANES), 0) + pl.program_id(1) * ts
    low = lane < HEAD_DIM
    one64 = jnp.where(lane == HEAD_DIM, 1.0, 0.0)
    ones3 = jnp.where(jnp.logical_and(lane >= HEAD_DIM, lane < HEAD_DIM + 3), 1.0, 0.0)
    blk_hot = jnp.where(lane - MOBA_LANE0 == (pos >> MOBA_SHIFT), 1.0, 0.0)
    posf = pos.astype(F32)
    pos_lanes = _pos_lanes(lane, pos)

    def head_tile(z, hd, extra):
        pair = z[:, (hd // 2) * LANES:(hd // 2 + 1) * LANES]
        if hd % 2:
            pair = pltpu.roll(pair, HEAD_DIM, 1)
        return jnp.where(low, pair, extra).astype(BF16)

    za = proj("A")
    for hd in range(N_HEADS):
        qA_ref[0, :, hd * LANES:(hd + 1) * LANES] = head_tile(
            za, hd, _pieces_lanes(lane, _split3_const(SLOPES_A[hd]), repeat=2))
    for kv in range(SWA_KV_HEADS):
        sl = slice(kv * LANES, (kv + 1) * LANES)
        kA_ref[0, :, sl] = head_tile(za, N_HEADS + kv, pos_lanes)
        vA_ref[0, :, sl] = head_tile(za, N_HEADS + SWA_KV_HEADS + kv, one64)

    zc = proj("C")
    for hd in range(N_HEADS):
        qC_ref[0, :, hd * LANES:(hd + 1) * LANES] = head_tile(
            zc, hd, _pieces_lanes(lane, _split3_const(SLOPES_C[hd]), repeat=2))
    kcvc_ref[0] = zc[:, 2 * LANES:3 * LANES].astype(BF16)
    kvC_ref[0, :, 0:LANES] = head_tile(zc, 6, pos_lanes)
    kvC_ref[0, :, LANES:2 * LANES] = jnp.where(lane == (pos >> NSA_SEL_SHIFT), 1.0, 0.0).astype(BF16)
    kvC_ref[0, :, 2 * LANES:3 * LANES] = head_tile(zc, 7, one64)
    kvC_ref[0, :, 3 * LANES:4 * LANES] = head_tile(zc, 8, pos_lanes)
    kvC_ref[0, :, 4 * LANES:5 * LANES] = head_tile(zc, 9, one64)
    gl_ref[0] = zc[:, 5 * LANES:6 * LANES]

    zd = proj("D")
    for hd in range(N_HEADS):
        sl = slice(hd * LANES, (hd + 1) * LANES)
        hi, mid, lo = _split3(posf * SLOPES_D[hd])
        alibi = jnp.where(lane == HEAD_DIM, hi.astype(F32),
                          jnp.where(lane == HEAD_DIM + 1, mid.astype(F32),
                                    jnp.where(lane == HEAD_DIM + 2, lo.astype(F32), 0.0)))
        qD_ref[0, :, sl] = head_tile(zd, hd, ones3)
        kD_ref[0, :, sl] = head_tile(zd, N_HEADS + hd, alibi + blk_hot)
        vD_ref[0, :, sl] = head_tile(zd, 2 * N_HEADS + hd, one64)

    zb = proj("mla")
    cq = zb[:, 0:256]
    ckv = zb[:, 256:384]
    kpm = zb[:, 384:512]
    kpr = zb[:, 512:640]
    cqn = cq * lax.rsqrt(jnp.sum(cq * cq, axis=-1, keepdims=True) * (1.0 / MLA_Q_RANK) + RMS_EPS) * qg_ref[...]
    ckvn = ckv * lax.rsqrt(jnp.mean(ckv * ckv, axis=-1, keepdims=True) + RMS_EPS) * kvg_ref[...]
    qall = _dot(cqn.astype(BF16), wq_ref[...])
    kvall = _dot(ckvn.astype(BF16), wkv_ref[...])
    cosq = cos_ref[...]
    sinq = sin_ref[...]
    kpe = kpm * cosq + kpr * sinq
    for hd in range(N_HEADS):
        lo = hd * 2 * LANES
        sl = slice(hd * LANES, (hd + 1) * LANES)
        qB_ref[0, :, sl] = (qall[:, lo:lo + LANES] * cosq + qall[:, lo + LANES:lo + 2 * LANES] * sinq).astype(BF16)
        kB_ref[0, :, sl] = (kvall[:, lo:lo + LANES] + kpe).astype(BF16)
        vB_ref[0, :, sl] = (kvall[:, lo + LANES:lo + 2 * LANES] + one64).astype(BF16)


def _in_proj(x, mod, g, wmain, qg, kvg, wq, wkv, cos_t, sin_t, ts=256):
    b, s, d = x.shape
    nt = s // ts
    tok = lambda w: pl.BlockSpec((1, ts, w), lambda i, j: (i, j, 0))
    full = lambda a: pl.BlockSpec(a.shape, lambda i, j: (0,) * a.ndim)
    outs = [("qB", 512, BF16), ("kB", 512, BF16), ("vB", 512, BF16), ("qA", 512, BF16), ("kA", 256, BF16),
            ("vA", 256, BF16), ("qC", 512, BF16), ("kcvc", 128, BF16), ("kvC", 640, BF16), ("gl", 128, F32),
            ("qD", 512, BF16), ("kD", 512, BF16), ("vD", 512, BF16)]
    res = pl.pallas_call(
        _in_proj_kernel,
        grid=(b, nt),
        in_specs=[tok(d), pl.BlockSpec((1, N_ADA, d), lambda i, j: (i, 0, 0)), full(g), full(wmain), full(qg),
                  full(kvg), full(wq), full(wkv),
                  pl.BlockSpec((ts, LANES), lambda i, j: (j, 0)), pl.BlockSpec((ts, LANES), lambda i, j: (j, 0))],
        out_specs=[tok(w) for _, w, _ in outs],
        out_shape=[jax.ShapeDtypeStruct((b, s, w), dt) for _, w, dt in outs],
        compiler_params=_cparams(("parallel", "parallel")),
        name="in_proj",
    )(x, mod, g, wmain, qg, kvg, wq, wkv, cos_t, sin_t)
    return {n: r for (n, _, _), r in zip(outs, res)}


def _fold_max(s):
    out = s[:, 0:LANES]
    for j in range(1, s.shape[1] // LANES):
        out = jnp.maximum(out, s[:, j * LANES:(j + 1) * LANES])
    return out


LOG2E = math.log2(math.e)


def _grouped_loops(n, groups, body, carry):
    pos = 0
    left = n
    for g in groups:
        cnt = left // g
        carry = lax.fori_loop(0, cnt, lambda i, cr, g=g, pos=pos: body(pos + i * g, g, cr), carry)
        pos = pos + cnt * g
        left = left - cnt * g
    return carry


def _two_pass(streams, n_past, tk, s_sc, scale=1.0, groups=(4, 2, 1), chunk_of=None):
    sc2 = scale * LOG2E

    def row0(c):
        return pl.multiple_of(c * tk, tk)

    def pass1(pos, g, mxs):
        out = []
        for i, st in enumerate(streams):
            if chunk_of is None:
                parts_src = [_dot_nt(st["q"], st["k_rows"](row0(pos), g * tk)) * sc2]
                parts = [parts_src[0][:, j * tk:(j + 1) * tk] for j in range(g)]
            else:
                parts = [_dot_nt(st["q"], st["k_rows"](row0(chunk_of(pos + j)), tk)) * sc2 for j in range(g)]
            mx = mxs[i]
            for j in range(g):
                s_sc[i, pos + j] = parts[j]
                mx = jnp.maximum(mx, _fold_max(parts[j]))
            out.append(mx)
        return tuple(out)

    own = [st["s_own"] * LOG2E for st in streams]
    mxs = _grouped_loops(n_past, groups, pass1, tuple(_fold_max(s) for s in own))
    m128 = [jnp.broadcast_to(jnp.max(mx, axis=1, keepdims=True), mx.shape) for mx in mxs]

    def p_of(s, mt):
        return jnp.exp2(s - jnp.concatenate([mt] * (s.shape[1] // LANES), axis=1)).astype(BF16)

    def pass2(pos, g, accs):
        out = []
        for i, st in enumerate(streams):
            if chunk_of is None:
                p = jnp.concatenate([p_of(s_sc[i, pos + j], m128[i]) for j in range(g)], axis=1)
                out.append(accs[i] + _dot(p, st["v_rows"](row0(pos), g * tk)))
            else:
                acc = accs[i]
                for j in range(g):
                    acc = acc + _dot(p_of(s_sc[i, pos + j], m128[i]), st["v_rows"](row0(chunk_of(pos + j)), tk))
                out.append(acc)
        return tuple(out)

    init = tuple(_dot(p_of(own[i], m128[i]), st["v_own"]) for i, st in enumerate(streams))
    return _grouped_loops(n_past, groups, pass2, init)


def _finish_ones(acc):
    return acc[:, :HEAD_DIM] * (1.0 / jnp.maximum(acc[:, HEAD_DIM:HEAD_DIM + 1], TINY))


def _dist(tq, tk, t0, k0):
    r = lax.broadcasted_iota(jnp.int32, (tq, tk), 0)
    c = lax.broadcasted_iota(jnp.int32, (tq, tk), 1)
    return (r - c) + (t0 - k0)


SWA_TQ = 256


def _swa_kernel(sink_ref, q_ref, k_ref, v_ref, o_ref):
    t0 = pl.program_id(1) * SWA_TQ
    slab = SWA_WINDOW + SWA_TQ
    k0 = pl.multiple_of(jnp.maximum(t0 - SWA_WINDOW, 0), SWA_WINDOW)
    dist = _dist(SWA_TQ, slab, t0, k0)
    ok = jnp.logical_and(dist >= 0, dist < SWA_WINDOW)
    tpos = (lax.broadcasted_iota(jnp.int32, (SWA_TQ, 1), 0) + t0).astype(F32)
    outs = []
    for hd in range(N_HEADS):
        kv = hd // (N_HEADS // SWA_KV_HEADS)
        ksl = slice(kv * LANES, (kv + 1) * LANES)
        s = jnp.where(ok, _dot_nt(q_ref[0, :, hd * LANES:(hd + 1) * LANES], k_ref[0, pl.ds(k0, slab), ksl]), NEG)
        sink = sink_ref[hd] + SLOPES_A[hd] * tpos
        m = jnp.maximum(jnp.max(s, axis=1, keepdims=True), sink)
        acc = _dot(jnp.exp(s - m).astype(BF16), v_ref[0, pl.ds(k0, slab), ksl])
        den = acc[:, HEAD_DIM:HEAD_DIM + 1] + jnp.exp(sink - m)
        outs.append(acc[:, :HEAD_DIM] * (1.0 / jnp.maximum(den, TINY)))
    o_ref[0] = jnp.concatenate(outs, axis=1).astype(o_ref.dtype)


def _swa(qA, kA, vA, sinks):
    b, s, _ = qA.shape
    assert s % SWA_TQ == 0 and s >= SWA_WINDOW + SWA_TQ
    tile = lambda w: pl.BlockSpec((1, SWA_TQ, w), lambda i, j: (i, j, 0))
    whole = lambda w: pl.BlockSpec((1, s, w), lambda i, j: (i, 0, 0))
    return pl.pallas_call(
        _swa_kernel,
        grid=(b, s // SWA_TQ),
        in_specs=[pl.BlockSpec(memory_space=pltpu.SMEM), tile(N_HEADS * LANES), whole(SWA_KV_HEADS * LANES),
                  whole(SWA_KV_HEADS * LANES)],
        out_specs=tile(N_HEADS * HEAD_DIM),
        out_shape=jax.ShapeDtypeStruct((b, s, N_HEADS * HEAD_DIM), BF16),
        compiler_params=_cparams(("parallel", "arbitrary")),
        name="swa",
    )(sinks, qA, kA, vA)


MLA_SCALE = float((MLA_NOPE + MLA_ROPE) ** -0.5)


def _rows_at(ref, sl):
    return lambda row0, n: ref[0, pl.ds(row0, n), sl]


def _mla_kernel(q_ref, k_ref, v_ref, o_ref, s_sc, *, tq, tk):
    t0 = pl.program_id(2) * tq
    n_past = t0 // tk
    k_own = pl.multiple_of(n_past * tk, tk)
    bias = jnp.where(_dist(tq, tk, t0, k_own) >= 0, 0.0, NEG)
    streams = []
    for hd in range(2):
        sl = slice(hd * LANES, (hd + 1) * LANES)
        q = q_ref[0, :, sl]
        s_own = _dot_nt(q, k_ref[0, pl.ds(k_own, tk), sl]) * MLA_SCALE + bias
        streams.append(dict(q=q, k_rows=_rows_at(k_ref, sl), v_rows=_rows_at(v_ref, sl), s_own=s_own,
                            v_own=v_ref[0, pl.ds(k_own, tk), sl]))
    accs = _two_pass(streams, n_past, tk, s_sc, scale=MLA_SCALE)
    o_ref[0] = jnp.concatenate([_finish_ones(acc) for acc in accs], axis=1).astype(o_ref.dtype)


def _mla(qB, kB, vB, tq=512, tk=512):
    b, s, _ = qB.shape
    tk = min(tk, s)
    nq = s // tq
    resident = lambda: pl.BlockSpec((1, s, 2 * LANES), lambda bi, p, i: (bi, 0, p), pipeline_mode=pl.Buffered(1))
    return pl.pallas_call(
        functools.partial(_mla_kernel, tq=tq, tk=tk),
        grid=(b, 2, nq),
        in_specs=[pl.BlockSpec((1, tq, 2 * LANES), lambda bi, p, i: (bi, i, p)), resident(), resident()],
        out_specs=pl.BlockSpec((1, tq, LANES), lambda bi, p, i: (bi, i, p)),
        out_shape=jax.ShapeDtypeStruct((b, s, N_HEADS * HEAD_DIM), BF16),
        scratch_shapes=[pltpu.VMEM((2, max(s // tk - 1, 1), tq, tk), F32)],
        compiler_params=_cparams(("parallel", "parallel", "arbitrary")),
        name="mla_attn",
    )(qB, kB, vB)


def _gelu_tanh(x):
    return x * (0.5 * (1.0 + jnp.tanh(math.sqrt(2.0 / math.pi) * (x + 0.044715 * (x * x * x)))))


def _nsa_cmp_kernel(ch_ref, w1_ref, pos_ref, w1f_ref, w2_ref, kc_ref, vc_ref):
    nc = ch_ref.shape[1]
    u = _dot(ch_ref[0], w1_ref[...])
    for idx, out in ((0, kc_ref), (1, vc_ref)):
        pos = jnp.broadcast_to(pos_ref[idx], (8, pos_ref.shape[2]))
        w1f = w1f_ref[idx]
        p_hi = pos.astype(BF16)
        p_lo = (pos - p_hi.astype(F32)).astype(BF16)
        w_hi = w1f.astype(BF16)
        w_lo = (w1f - w_hi.astype(F32)).astype(BF16)
        posb = (_dot(p_hi, w_hi) + _dot(p_hi, w_lo) + _dot(p_lo, w_hi))[0:1, :]
        top = u[:, (2 * idx) * LANES:(2 * idx + 1) * LANES]
        bot = u[:, (2 * idx + 1) * LANES:(2 * idx + 2) * LANES]
        hid = top + pltpu.roll(bot, nc - 1, 0) + posb
        res = _dot(_gelu_tanh(hid).astype(BF16), w2_ref[idx])
        if idx == 0:
            lane = lax.broadcasted_iota(jnp.int32, (nc, LANES), 1)
            cend = lax.broadcasted_iota(jnp.int32, (nc, LANES), 0) * NSA_CMP_STRIDE + (NSA_CMP_LEN - 1)
            res = res + _pos_lanes(lane, cend)
        out[0] = res.astype(out.dtype)


def _nsa_compress(ch, w1aug, pos2, w1f, w2p):
    b, nc, w = ch.shape
    full = lambda a: pl.BlockSpec(a.shape, lambda i: (0,) * a.ndim)
    return pl.pallas_call(
        _nsa_cmp_kernel,
        grid=(b,),
        in_specs=[pl.BlockSpec((1, nc, w), lambda i: (i, 0, 0)), full(w1aug), full(pos2), full(w1f), full(w2p)],
        out_specs=[pl.BlockSpec((1, nc, LANES), lambda i: (i, 0, 0))] * 2,
        out_shape=[jax.ShapeDtypeStruct((b, nc, LANES), BF16)] * 2,
        compiler_params=_cparams(("parallel",)),
        name="nsa_compress",
    )(ch, w1aug, pos2, w1f, w2p)


NSA_CHUNK = 256


def _nsa_kernel(q_ref, kc_ref, vc_ref, kv_ref, gl_ref, mm_ref, o_ref, imp_sc, s_sc, act_sm):
    n = pl.program_id(1)
    t0 = n * QB
    nc = kc_ref.shape[1]
    nbl = LANES
    qh = [q_ref[0, :, hd * LANES:(hd + 1) * LANES] for hd in range(N_HEADS)]

    def stack(x):
        return jnp.concatenate([x] * N_HEADS, axis=0)

    qs = jnp.concatenate(qh, axis=0)

    slab = NSA_WINDOW + QB
    k0w = pl.multiple_of(jnp.maximum(n - NSA_WINDOW // QB, 0) * QB, QB)
    dist = _dist(QB, slab, t0, k0w)
    rblk = 64
    n_rblk = N_HEADS * QB // rblk
    bias_w = jnp.where(jnp.logical_and(dist >= 0, dist < NSA_WINDOW), 0.0, NEG)
    s = _dot_nt(qs, kv_ref[0, pl.ds(k0w, slab), 3 * LANES:4 * LANES])
    blocks = []
    for rb in range(n_rblk):
        qo = (rb * rblk) % QB
        blk = s[rb * rblk:(rb + 1) * rblk] + bias_w[qo:qo + rblk]
        blocks.append(jnp.exp(blk - jnp.max(blk, axis=1, keepdims=True)).astype(BF16))
    o_win = _finish_ones(_dot(jnp.concatenate(blocks, axis=0), kv_ref[0, pl.ds(k0w, slab), 4 * LANES:5 * LANES]))

    tq1 = lax.broadcasted_iota(jnp.int32, (QB, 1), 0) + t0
    cidx = lax.broadcasted_iota(jnp.int32, (QB, nc), 1)
    ok = jnp.logical_and(tq1 >= cidx * NSA_CMP_STRIDE + (NSA_CMP_LEN - 1), cidx < nc - 1)
    bias_c = jnp.where(ok, 0.0, NEG)
    row_ok = jnp.where(tq1 >= NSA_CMP_LEN - 1, 1.0, 0.0)
    s = _dot_nt(qs, kc_ref[0])
    blocks = []
    psum_parts = [None] * (QB // rblk)
    for rb in range(n_rblk):
        qo = (rb * rblk) % QB
        blk = s[rb * rblk:(rb + 1) * rblk] + bias_c[qo:qo + rblk]
        e = jnp.exp(blk - jnp.max(blk, axis=1, keepdims=True))
        p = e * (row_ok[qo:qo + rblk] / jnp.maximum(jnp.sum(e, axis=1, keepdims=True), TINY))
        blocks.append(p.astype(BF16))
        part = psum_parts[qo // rblk]
        psum_parts[qo // rblk] = p if part is None else part + p
    o_cmp = _dot(jnp.concatenate(blocks, axis=0), vc_ref[0])
    psum = jnp.concatenate(psum_parts, axis=0)

    mm = mm_ref[...]
    imp = None
    for piece in _split3(psum):
        d = _dot(piece, mm)
        imp = d if imp is None else imp + d
    imp_t = imp.T
    jrow = lax.broadcasted_iota(jnp.int32, (nbl, QB), 0)
    tcol = lax.broadcasted_iota(jnp.int32, (nbl, QB), 1) + t0
    cur = tcol >> 6
    cand = jrow <= cur
    forced = jnp.logical_or(jnp.logical_or(jrow == 0, jrow == cur), jrow == cur - 1)
    a = jnp.where(cand, jnp.where(forced, BIG, imp_t), NEG)
    key = lax.bitcast_convert_type(a, jnp.int32)
    key1 = key + 1
    imp_sc[...] = key
    n_cand = (t0 + QB - 1) // NSA_SEL_BLOCK + 1

    def rank_body(i2, cnt):
        for u in range(2):
            i = 2 * i2 + u
            row = imp_sc[pl.ds(i, 1), :]
            cnt = cnt + jnp.where(row >= jnp.where(jrow <= i, key1, key), 1.0, 0.0)
        return cnt

    rank = lax.fori_loop(0, n_cand // 2, rank_body, jnp.zeros((nbl, QB), F32))
    sel_t = jnp.where(jnp.logical_and(rank < NSA_TOPN, cand), 0.0, NEG)
    selneg = sel_t.T.astype(BF16)

    q_sel = jnp.concatenate([jnp.concatenate([qh[hd], selneg], axis=1) for hd in range(N_HEADS)], axis=0)
    own = t0 // NSA_CHUNK
    k_own = pl.multiple_of(own * NSA_CHUNK, NSA_CHUNK)
    ksl, vsl = slice(0, 2 * LANES), slice(2 * LANES, 3 * LANES)
    s_own = (_dot_nt(q_sel, kv_ref[0, pl.ds(k_own, NSA_CHUNK), ksl])
             + stack(jnp.where(_dist(QB, NSA_CHUNK, t0, k_own) >= 0, 0.0, NEG)))
    n_chunks = kv_ref.shape[1] // NSA_CHUNK
    blk_shift = (NSA_CHUNK // NSA_SEL_BLOCK).bit_length() - 1
    grp = jnp.where((lax.broadcasted_iota(jnp.int32, (n_chunks, nbl), 1) >> blk_shift)
                    == lax.broadcasted_iota(jnp.int32, (n_chunks, nbl), 0), 1.0, 0.0).astype(BF16)
    picked = jnp.where(sel_t == 0.0, 1.0, 0.0).astype(BF16)
    cnt = _dot(_dot(grp, picked).astype(BF16), jnp.ones((QB, LANES), BF16))
    n_act = jnp.int32(0)
    for c in range(n_chunks):
        act_sm[n_act] = c
        n_act = n_act + jnp.where(jnp.logical_and(cnt[c, 0] > 0.0, c < own), 1, 0)
    stream = dict(q=q_sel, k_rows=_rows_at(kv_ref, ksl), v_rows=_rows_at(kv_ref, vsl),
                  s_own=s_own, v_own=kv_ref[0, pl.ds(k_own, NSA_CHUNK), vsl])
    o_sel = _finish_ones(_two_pass([stream], n_act, NSA_CHUNK, s_sc, groups=(4, 2, 1),
                                   chunk_of=lambda pos: act_sm[pos])[0])

    g = jax.nn.sigmoid(gl_ref[0])
    outs = []
    for hd in range(N_HEADS):
        rs = slice(hd * QB, (hd + 1) * QB)
        outs.append(g[:, hd:hd + 1] * o_cmp[rs, :HEAD_DIM] + g[:, N_HEADS + hd:N_HEADS + hd + 1] * o_sel[rs]
                    + g[:, 2 * N_HEADS + hd:2 * N_HEADS + hd + 1] * o_win[rs])
    o_ref[0] = jnp.concatenate(outs, axis=1).astype(o_ref.dtype)


def _nsa(qC, kcmp, vcmp, kvC, gl, m_mat):
    b, s, _ = qC.shape
    nq = s // QB
    nc = kcmp.shape[1]
    assert s >= NSA_WINDOW + QB and s % NSA_CHUNK == 0
    return pl.pallas_call(
        _nsa_kernel,
        grid=(b, nq),
        in_specs=[pl.BlockSpec((1, QB, N_HEADS * LANES), lambda i, j: (i, j, 0)),
                  pl.BlockSpec((1, nc, LANES), lambda i, j: (i, 0, 0)),
                  pl.BlockSpec((1, nc, LANES), lambda i, j: (i, 0, 0)),
                  pl.BlockSpec((1, s, 5 * LANES), lambda i, j: (i, 0, 0)),
                  pl.BlockSpec((1, QB, LANES), lambda i, j: (i, j, 0)),
                  pl.BlockSpec(m_mat.shape, lambda i, j: (0, 0))],
        out_specs=pl.BlockSpec((1, QB, N_HEADS * HEAD_DIM), lambda i, j: (i, j, 0)),
        out_shape=jax.ShapeDtypeStruct((b, s, N_HEADS * HEAD_DIM), BF16),
        scratch_shapes=[pltpu.VMEM((LANES, QB), jnp.int32),
                        pltpu.VMEM((1, max(s // NSA_CHUNK - 1, 1), N_HEADS * QB, NSA_CHUNK), F32),
                        pltpu.SMEM((s // NSA_CHUNK + 1,), jnp.int32)],
        compiler_params=_cparams(("parallel", "arbitrary")),
        name="nsa_attn",
    )(qC, kcmp, vcmp, kvC, gl, m_mat)


MOBA_NBR = LANES - MOBA_LANE0


MOBA_TQ = 2 * MOBA_BLOCK


def _moba_kernel(q_ref, k_ref, v_ref, o_ref, kmean_sc, gate_sc, s_sc):
    tq = MOBA_TQ
    n = pl.program_id(2)
    t0 = n * tq
    n_blk = k_ref.shape[1] // MOBA_BLOCK
    lane = lax.broadcasted_iota(jnp.int32, (1, LANES), 1)

    @pl.when(n == 0)
    def _():
        kmean_sc[...] = jnp.zeros(kmean_sc.shape, F32)
        for hd in range(2):
            for jb in range(n_blk):
                blk = k_ref[0, jb * MOBA_BLOCK:(jb + 1) * MOBA_BLOCK, hd * LANES:(hd + 1) * LANES].astype(F32)
                kmean_sc[hd, jb:jb + 1, :] = jnp.where(lane < HEAD_DIM, jnp.mean(blk, axis=0, keepdims=True), 0.0)

    jrow = lax.broadcasted_iota(jnp.int32, (MOBA_NBR, tq), 0)
    qcol = lax.broadcasted_iota(jnp.int32, (MOBA_NBR, tq), 1)
    cur = 2 * n + (qcol >> MOBA_SHIFT)
    qs = [q_ref[0, :, hd * LANES:(hd + 1) * LANES] for hd in range(2)]
    gates = []
    for hd in range(2):
        gate_t = _dot_nt(kmean_sc[hd].astype(BF16), qs[hd])[0:MOBA_NBR, :]
        a = jnp.where(jrow < cur, gate_t, NEG)
        gate_sc[hd] = a
        gates.append(a)

    def rank_body(i, cnts):
        out = []
        for hd in range(2):
            row = gate_sc[hd, pl.ds(i, 1), :]
            c_ge = jnp.where(row >= gates[hd], 1.0, 0.0)
            c_gt = jnp.where(row > gates[hd], 1.0, 0.0)
            out.append(cnts[hd] + jnp.where(jrow > i, c_ge, c_gt))
        return tuple(out)

    ranks = lax.fori_loop(0, 2 * n + 1, rank_body, tuple(jnp.zeros((MOBA_NBR, tq), F32) for _ in range(2)))
    tk = MOBA_TQ
    k_own = pl.multiple_of(t0, tk)
    bias = jnp.where(_dist(tq, tk, 0, 0) >= 0, 0.0, NEG)
    streams = []
    for hd in range(2):
        sl = slice(hd * LANES, (hd + 1) * LANES)
        keep = jnp.logical_or(jnp.logical_and(ranks[hd] < MOBA_TOPK, jrow < cur), jrow == cur)
        full_t = jnp.concatenate([jnp.zeros((MOBA_LANE0, tq), F32), jnp.where(keep, 0.0, NEG)], axis=0)
        q_sel = (qs[hd].astype(F32) + full_t.T).astype(BF16)
        s_own = _dot_nt(q_sel, k_ref[0, pl.ds(k_own, tk), sl]) + bias
        streams.append(dict(q=q_sel, k_rows=_rows_at(k_ref, sl), v_rows=_rows_at(v_ref, sl), s_own=s_own,
                            v_own=v_ref[0, pl.ds(k_own, tk), sl]))
    accs = _two_pass(streams, n, tk, s_sc)
    o_ref[0] = jnp.concatenate([_finish_ones(acc) for acc in accs], axis=1).astype(o_ref.dtype)


def _moba(qD, kD, vD):
    b, s, _ = qD.shape
    tq = MOBA_TQ
    assert s % tq == 0 and s // MOBA_BLOCK <= MOBA_NBR
    resident = lambda: pl.BlockSpec((1, s, 2 * LANES), lambda i, p, j: (i, 0, p), pipeline_mode=pl.Buffered(1))
    return pl.pallas_call(
        _moba_kernel,
        grid=(b, 2, s // tq),
        in_specs=[pl.BlockSpec((1, tq, 2 * LANES), lambda i, p, j: (i, j, p)), resident(), resident()],
        out_specs=pl.BlockSpec((1, tq, LANES), lambda i, p, j: (i, j, p)),
        out_shape=jax.ShapeDtypeStruct((b, s, N_HEADS * HEAD_DIM), BF16),
        scratch_shapes=[pltpu.VMEM((2, LANES, LANES), F32), pltpu.VMEM((2, MOBA_NBR, tq), F32),
                        pltpu.VMEM((2, max(s // tq - 1, 1), tq, tq), F32)],
        compiler_params=_cparams(("parallel", "parallel", "arbitrary")),
        name="moba_attn",
    )(qD, kD, vD)


def _post_kernel(oa_ref, ob_ref, oc_ref, od_ref, x_ref, mod_ref, wo_ref, g_ref, wu_ref, wd_ref, fg_ref, out_ref,
                 *, final, ffc):
    x = x_ref[0]
    mix = None
    for idx, ref in enumerate((oa_ref, ob_ref, oc_ref, od_ref)):
        d = _dot(ref[0], wo_ref[idx])
        mix = d if mix is None else mix + d
    x1 = x + mod_ref[0, 2:3, :] * mix
    y = x1 * lax.rsqrt(jnp.mean(x1 * x1, axis=-1, keepdims=True) + RMS_EPS) * g_ref[...]
    hb = (y * (1.0 + mod_ref[0, 4:5, :]) + mod_ref[0, 3:4, :]).astype(BF16)
    acc = None
    for c in range(wu_ref.shape[1] // ffc):
        hid = jnp.maximum(_dot(hb, wu_ref[:, c * ffc:(c + 1) * ffc]), 0.0)
        d = _dot((hid * hid).astype(BF16), wd_ref[c * ffc:(c + 1) * ffc, :])
        acc = d if acc is None else acc + d
    x2 = x1 + mod_ref[0, 5:6, :] * acc
    if final:
        x2 = x2 * lax.rsqrt(jnp.mean(x2 * x2, axis=-1, keepdims=True) + RMS_EPS) * fg_ref[...]
    out_ref[0] = x2


def _post(oa, ob, oc, od, x, mod, wo4, g, wu, wd, fg, final, ts=256, ffc=1024):
    b, s, d = x.shape
    nt = s // ts
    tok = lambda w: pl.BlockSpec((1, ts, w), lambda i, j: (i, j, 0))
    full = lambda a: pl.BlockSpec(a.shape, lambda i, j: (0,) * a.ndim, pipeline_mode=pl.Buffered(1))
    hw = N_HEADS * HEAD_DIM
    return pl.pallas_call(
        functools.partial(_post_kernel, final=final, ffc=ffc),
        grid=(b, nt),
        in_specs=[tok(hw), tok(hw), tok(hw), tok(hw), tok(d), pl.BlockSpec((1, N_ADA, d), lambda i, j: (i, 0, 0)),
                  full(wo4), full(g), full(wu), full(wd), full(fg)],
        out_specs=tok(d),
        out_shape=jax.ShapeDtypeStruct((b, s, d), F32),
        compiler_params=_cparams(("parallel", "parallel")),
        name="post_mlp",
    )(oa, ob, oc, od, x, mod, wo4, g, wu, wd, fg)


def _rot_half_cols(w):
    half = w.shape[1] // 2
    return jnp.concatenate([-w[:, half:], w[:, :half]], axis=1)


def _layer_weights(w_in, mla_qg, mla_kvg, w_uq, w_ukv, pos_k, pos_v, ck_w1, ck_w2, cv_w1, cv_w2):
    d = w_in.shape[0]
    z = lambda n: jnp.zeros((d, n), F32)
    o = 0

    def take(n):
        nonlocal o
        r = w_in[:, o:o + n]
        o += n
        return r

    qa, ka, va = take(256), take(128), take(128)
    cq, ckv, kpe = take(MLA_Q_RANK), take(MLA_KV_RANK), take(MLA_ROPE)
    qc, kc, vc, ks, vs, kw, vw, gl = take(256), take(64), take(64), take(64), take(64), take(64), take(64), take(12)
    qd, kd, vd = take(256), take(256), take(256)
    sc = 1.0 / 8.0
    kpe_main = jnp.concatenate([z(64), kpe, z(32)], axis=1)
    kpe_rot = jnp.concatenate([z(64), _rot_half_cols(kpe), z(32)], axis=1)
    p64 = lambda w: jnp.pad(w, ((0, 0), (0, LANES - w.shape[1])))
    groups = {
        "mla": jnp.concatenate([cq, z(64), ckv, kpe_main, kpe_rot], axis=1),
        "A": jnp.concatenate([qa * sc, ka, va], axis=1),
        "C": jnp.concatenate([qc * sc, kc, vc, ks, vs, kw, vw, p64(gl)], axis=1),
        "D": jnp.concatenate([qd * sc, kd, vd], axis=1),
    }
    wmain = jnp.concatenate([groups[n] for n, _ in _IN_GROUPS], axis=1).astype(BF16)

    dq = MLA_NOPE + MLA_ROPE
    wq_cols, wkv_cols = [], []
    for hd in range(N_HEADS):
        wh = w_uq[:, hd * dq:(hd + 1) * dq]
        zq = lambda n: jnp.zeros((MLA_Q_RANK, n), F32)
        wq_cols += [wh, zq(32), zq(64), _rot_half_cols(wh[:, MLA_NOPE:]), zq(32)]
        wk = w_ukv[:, hd * 128:(hd + 1) * 128]
        zk = jnp.zeros((MLA_KV_RANK, 64), F32)
        wkv_cols += [wk[:, :64], zk, wk[:, 64:], zk]
    wq = jnp.pad(jnp.concatenate(wq_cols, axis=1), ((0, 256 - MLA_Q_RANK), (0, 0))).astype(BF16)
    wkv = jnp.concatenate(wkv_cols, axis=1).astype(BF16)
    qg = jnp.pad(mla_qg, (0, 256 - MLA_Q_RANK)).reshape(1, 256)
    kvg = mla_kvg.reshape(1, MLA_KV_RANK)


    def aug(w1, is_v):
        w = w1.reshape(NSA_CMP_LEN, HEAD_DIM, NSA_CMP_HIDDEN)
        zz = jnp.zeros_like(w)
        w = jnp.concatenate([zz, w] if is_v else [w, zz], axis=1)
        w = w.reshape(NSA_CMP_LEN * LANES, NSA_CMP_HIDDEN)
        return w[:NSA_CMP_STRIDE * LANES], w[NSA_CMP_STRIDE * LANES:]

    kt, kb = aug(ck_w1, False)
    vt, vb = aug(cv_w1, True)
    w1aug = jnp.concatenate([kt, kb, vt, vb], axis=1).astype(BF16)
    pos2 = jnp.stack([pos_k.reshape(1, -1), pos_v.reshape(1, -1)])
    w1f = jnp.stack([ck_w1, cv_w1])
    w2p = jnp.stack([p64(ck_w2), p64(cv_w2)]).astype(BF16)
    return wmain, qg, kvg, wq, wkv, w1aug, pos2, w1f, w2p


def _rope_tables(s):
    half = MLA_ROPE // 2
    freqs = ROPE_THETA ** (-jnp.arange(half, dtype=F32) / half)
    ang = jnp.arange(s, dtype=F32)[:, None] * freqs[None, :]
    cos, sin = jnp.cos(ang), jnp.sin(ang)
    ones, zeros = jnp.ones((s, MLA_NOPE), F32), jnp.zeros((s, MLA_NOPE), F32)
    tail = jnp.zeros((s, LANES - MLA_NOPE - MLA_ROPE), F32)
    return (jnp.concatenate([ones, cos, cos, tail], axis=1), jnp.concatenate([zeros, sin, sin, tail], axis=1))


def _nsa_imp_matrix(s):
    n_sel = s // NSA_SEL_BLOCK
    nc = s // NSA_CMP_STRIDE
    assert n_sel <= LANES
    c = np.arange(nc)[:, None]
    j = np.arange(LANES)[None, :]
    m = ((c >= 4 * j - 1) & (c <= 4 * j + 3) & (c < nc - 1) & (j < n_sel)).astype(np.float32)
    return jnp.asarray(m, BF16)


def kernel(x, c, norm_mix_g, norm_mlp_g, w_ada, b_ada, w_in, w_out, swa_sinks, mla_q_norm_g, mla_kv_norm_g,
           mla_w_uq, mla_w_ukv, nsa_cmp_pos_k, nsa_cmp_pos_v, nsa_cmp_k_w1, nsa_cmp_k_w2, nsa_cmp_v_w1,
           nsa_cmp_v_w2, w_up, w_down, final_norm_g):
    b, s, d = x.shape
    depth = w_in.shape[0]
    assert b <= SUBLANES
    c8 = jnp.pad(c, ((0, SUBLANES - b), (0, 0)))
    mod_all = _ada(c8, w_ada, b_ada)
    cos_t, sin_t = _rope_tables(s)
    m_mat = _nsa_imp_matrix(s)
    fg = final_norm_g.reshape(1, d)
    for l in range(depth):
        mod = mod_all[l, :b].reshape(b, N_ADA, d)
        wmain, qg, kvg, wq, wkv, w1aug, pos2, w1f, w2p = _layer_weights(
            w_in[l], mla_q_norm_g[l], mla_kv_norm_g[l], mla_w_uq[l], mla_w_ukv[l], nsa_cmp_pos_k[l],
            nsa_cmp_pos_v[l], nsa_cmp_k_w1[l], nsa_cmp_k_w2[l], nsa_cmp_v_w1[l], nsa_cmp_v_w2[l])
        z = _in_proj(x, mod, norm_mix_g[l].reshape(1, d), wmain, qg, kvg, wq, wkv, cos_t, sin_t)
        o_a = _swa(z["qA"], z["kA"], z["vA"], swa_sinks[l])
        o_b = _mla(z["qB"], z["kB"], z["vB"])
        ch = z["kcvc"].reshape(b, s // NSA_CMP_STRIDE, NSA_CMP_STRIDE * LANES)
        kcmp, vcmp = _nsa_compress(ch, w1aug, pos2, w1f, w2p)
        o_c = _nsa(z["qC"], kcmp, vcmp, z["kvC"], z["gl"], m_mat)
        o_d = _moba(z["qD"], z["kD"], z["vD"])
        wo4 = w_out[l].astype(BF16).reshape(4, N_HEADS * HEAD_DIM, d)
        x = _post(o_a, o_b, o_c, o_d, x, mod, wo4, norm_mlp_g[l].reshape(1, d), w_up[l].astype(BF16),
                  w_down[l].astype(BF16), fg, final=(l == depth - 1))
    return x
```

```python
import functools
import math

import numpy as np
import jax
import jax.numpy as jnp
from jax import lax
from jax.experimental import pallas as pl
from jax.experimental.pallas import tpu as pltpu

F32 = jnp.float32
BF16 = jnp.bfloat16

HEAD_DIM = 64
LANES = 128
SUBLANES = 8
QB = 256
NEG = -1e30
TINY = 1e-30
BIG = 1e9
RMS_EPS = 1e-6
N_ADA = 6
N_HEADS = 4

SWA_KV_HEADS = 2
SWA_WINDOW = 128
MLA_Q_RANK = 192
MLA_KV_RANK = 128
MLA_NOPE = 64
MLA_ROPE = 32
ROPE_THETA = 10000.0
NSA_CMP_LEN = 32
NSA_CMP_STRIDE = 16
NSA_CMP_HIDDEN = 128
NSA_SEL_BLOCK = 64
NSA_SEL_SHIFT = 6
NSA_TOPN = 16
NSA_WINDOW = 512
MOBA_BLOCK = 256
MOBA_SHIFT = 8
MOBA_TOPK = 3
MOBA_LANE0 = 96
N_ALIBI = 3 * N_HEADS

_S_ALL = [2.0 ** (-8.0 * (i + 1) / N_ALIBI) for i in range(N_ALIBI)]
SLOPES_A = [float(np.float32(v)) for v in _S_ALL[0::3]]
SLOPES_C = [float(np.float32(v)) for v in _S_ALL[1::3]]
SLOPES_D = [float(np.float32(v)) for v in _S_ALL[2::3]]

VMEM_LIMIT = 56 * 1024 * 1024

_NT = (((1,), (1,)), ((), ()))


def _cparams(sem):
    return pltpu.CompilerParams(dimension_semantics=sem, vmem_limit_bytes=VMEM_LIMIT)


def _split3(a):
    hi = a.astype(BF16)
    r1 = a - hi.astype(F32)
    mid = r1.astype(BF16)
    lo = (r1 - mid.astype(F32)).astype(BF16)
    return hi, mid, lo


def _dot(a, b):
    return jnp.dot(a, b, preferred_element_type=F32)


def _dot_nt(a, b):
    return lax.dot_general(a, b, _NT, preferred_element_type=F32)


def _split3_const(x):
    x = np.float32(x)
    hi = np.float32(np.asarray(x, dtype=BF16))
    mid = np.float32(np.asarray(np.float32(x - hi), dtype=BF16))
    lo = np.float32(np.asarray(np.float32(x - hi - mid), dtype=BF16))
    return float(hi), float(mid), float(lo)


def _pieces_lanes(lane, pieces, repeat):
    out = jnp.zeros(lane.shape, F32)
    for i, pc in enumerate(pieces):
        lo = HEAD_DIM + i * repeat
        out = jnp.where(jnp.logical_and(lane >= lo, lane < lo + repeat), pc, out)
    return out


def _pos_lanes(lane, pos):
    hi = ((pos >> 7) << 7).astype(F32)
    lo = (pos & 127).astype(F32)
    k = lane - HEAD_DIM
    return jnp.where(jnp.logical_and(k >= 0, k < 6), jnp.where((k & 1) == 0, hi, lo), 0.0)


def _ada_kernel(c_ref, w_ref, b_ref, o_ref):
    c = c_ref[...]
    a = c * jax.nn.sigmoid(c)
    w = w_ref[0]
    a_hi = a.astype(BF16)
    a_lo = (a - a_hi.astype(F32)).astype(BF16)
    w_hi = w.astype(BF16)
    w_lo = (w - w_hi.astype(F32)).astype(BF16)
    o_ref[0] = _dot(a_hi, w_hi) + _dot(a_hi, w_lo) + _dot(a_lo, w_hi) + b_ref[0]


def _ada(c8, w_ada, b_ada):
    depth, d, n = w_ada.shape
    nb = n // d
    return pl.pallas_call(
        _ada_kernel,
        grid=(depth, nb),
        in_specs=[
            pl.BlockSpec((SUBLANES, d), lambda l, j: (0, 0)),
            pl.BlockSpec((1, d, d), lambda l, j: (l, 0, j)),
            pl.BlockSpec((1, 1, d), lambda l, j: (l, 0, j)),
        ],
        out_specs=pl.BlockSpec((1, SUBLANES, d), lambda l, j: (l, 0, j)),
        out_shape=jax.ShapeDtypeStruct((depth, SUBLANES, n), F32),
        compiler_params=_cparams(("parallel", "parallel")),
        name="ada_mod",
    )(c8, w_ada, b_ada.reshape(depth, 1, n))


_IN_GROUPS = (("mla", 640), ("A", 512), ("C", 768), ("D", 768))
_IN_OFFS = {}
_o = 0
for _n, _w in _IN_GROUPS:
    _IN_OFFS[_n] = (_o, _o + _w)
    _o += _w


def _in_proj_kernel(x_ref, mod_ref, g_ref, w_ref, qg_ref, kvg_ref, wq_ref, wkv_ref, cos_ref, sin_ref,
                    qB_ref, kB_ref, vB_ref, qA_ref, kA_ref, vA_ref, qC_ref, kcvc_ref, kvC_ref, gl_ref,
                    qD_ref, kD_ref, vD_ref):
    x = x_ref[0]
    y = x * lax.rsqrt(jnp.mean(x * x, axis=-1, keepdims=True) + RMS_EPS) * g_ref[...]
    h = y * (1.0 + mod_ref[0, 1:2, :]) + mod_ref[0, 0:1, :]
    hb = h.astype(BF16)

    def proj(name):
        a, b = _IN_OFFS[name]
        return _dot(hb, w_ref[:, a:b])

    ts = x.shape[0]
    lane = lax.broadcasted_iota(jnp.int32, (ts, LANES), 1)
    pos = lax.broadcasted_iota(jnp.int32, (ts, LANES), 0) + pl.program_id(1) * ts
    low = lane < HEAD_DIM
    one64 = jnp.where(lane == HEAD_DIM, 1.0, 0.0)
    ones3 = jnp.where(jnp.logical_and(lane >= HEAD_DIM, lane < HEAD_DIM + 3), 1.0, 0.0)
    blk_hot = jnp.where(lane - MOBA_LANE0 == (pos >> MOBA_SHIFT), 1.0, 0.0)
    posf = pos.astype(F32)
    pos_lanes = _pos_lanes(lane, pos)

    def head_tile(z, hd, extra):
        pair = z[:, (hd // 2) * LANES:(hd // 2 + 1) * LANES]
        if hd % 2:
            pair = pltpu.roll(pair, HEAD_DIM, 1)
        return jnp.where(low, pair, extra).astype(BF16)

    za = proj("A")
    for hd in range(N_HEADS):
        qA_ref[0, :, hd * LANES:(hd + 1) * LANES] = head_tile(
            za, hd, _pieces_lanes(lane, _split3_const(SLOPES_A[hd]), repeat=2))
    for kv in range(SWA_KV_HEADS):
        sl = slice(kv * LANES, (kv + 1) * LANES)
        kA_ref[0, :, sl] = head_tile(za, N_HEADS + kv, pos_lanes)
        vA_ref[0, :, sl] = head_tile(za, N_HEADS + SWA_KV_HEADS + kv, one64)

    zc = proj("C")
    for hd in range(N_HEADS):
        qC_ref[0, :, hd * LANES:(hd + 1) * LANES] = head_tile(
            zc, hd, _pieces_lanes(lane, _split3_const(SLOPES_C[hd]), repeat=2))
    kcvc_ref[0] = zc[:, 2 * LANES:3 * LANES].astype(BF16)
    kvC_ref[0, :, 0:LANES] = head_tile(zc, 6, pos_lanes)
    kvC_ref[0, :, LANES:2 * LANES] = jnp.where(lane == (pos >> NSA_SEL_SHIFT), 1.0, 0.0).astype(BF16)
    kvC_ref[0, :, 2 * LANES:3 * LANES] = head_tile(zc, 7, one64)
    kvC_ref[0, :, 3 * LANES:4 * LANES] = head_tile(zc, 8, pos_lanes)
    kvC_ref[0, :, 4 * LANES:5 * LANES] = head_tile(zc, 9, one64)
    gl_ref[0] = zc[:, 5 * LANES:6 * LANES]

    zd = proj("D")
    for hd in range(N_HEADS):
        sl = slice(hd * LANES, (hd + 1) * LANES)
        hi, mid, lo = _split3(posf * SLOPES_D[hd])
        alibi = jnp.where(lane == HEAD_DIM, hi.astype(F32),
                          jnp.where(lane == HEAD_DIM + 1, mid.astype(F32),
                                    jnp.where(lane == HEAD_DIM + 2, lo.astype(F32), 0.0)))
        qD_ref[0, :, sl] = head_tile(zd, hd, ones3)
        kD_ref[0, :, sl] = head_tile(zd, N_HEADS + hd, alibi + blk_hot)
        vD_ref[0, :, sl] = head_tile(zd, 2 * N_HEADS + hd, one64)

    zb = proj("mla")
    cq = zb[:, 0:256]
    ckv = zb[:, 256:384]
    kpm = zb[:, 384:512]
    kpr = zb[:, 512:640]
    cqn = cq * lax.rsqrt(jnp.sum(cq * cq, axis=-1, keepdims=True) * (1.0 / MLA_Q_RANK) + RMS_EPS) * qg_ref[...]
    ckvn = ckv * lax.rsqrt(jnp.mean(ckv * ckv, axis=-1, keepdims=True) + RMS_EPS) * kvg_ref[...]
    qall = _dot(cqn.astype(BF16), wq_ref[...])
    kvall = _dot(ckvn.astype(BF16), wkv_ref[...])
    cosq = cos_ref[...]
    sinq = sin_ref[...]
    kpe = kpm * cosq + kpr * sinq
    for hd in range(N_HEADS):
        lo = hd * 2 * LANES
        sl = slice(hd * LANES, (hd + 1) * LANES)
        qB_ref[0, :, sl] = (qall[:, lo:lo + LANES] * cosq + qall[:, lo + LANES:lo + 2 * LANES] * sinq).astype(BF16)
        kB_ref[0, :, sl] = (kvall[:, lo:lo + LANES] + kpe).astype(BF16)
        vB_ref[0, :, sl] = (kvall[:, lo + LANES:lo + 2 * LANES] + one64).astype(BF16)


def _in_proj(x, mod, g, wmain, qg, kvg, wq, wkv, cos_t, sin_t, ts=256):
    b, s, d = x.shape
    nt = s // ts
    tok = lambda w: pl.BlockSpec((1, ts, w), lambda i, j: (i, j, 0))
    full = lambda a: pl.BlockSpec(a.shape, lambda i, j: (0,) * a.ndim)
    outs = [("qB", 512, BF16), ("kB", 512, BF16), ("vB", 512, BF16), ("qA", 512, BF16), ("kA", 256, BF16),
            ("vA", 256, BF16), ("qC", 512, BF16), ("kcvc", 128, BF16), ("kvC", 640, BF16), ("gl", 128, F32),
            ("qD", 512, BF16), ("kD", 512, BF16), ("vD", 512, BF16)]
    res = pl.pallas_call(
        _in_proj_kernel,
        grid=(b, nt),
        in_specs=[tok(d), pl.BlockSpec((1, N_ADA, d), lambda i, j: (i, 0, 0)), full(g), full(wmain), full(qg),
                  full(kvg), full(wq), full(wkv),
                  pl.BlockSpec((ts, LANES), lambda i, j: (j, 0)), pl.BlockSpec((ts, LANES), lambda i, j: (j, 0))],
        out_specs=[tok(w) for _, w, _ in outs],
        out_shape=[jax.ShapeDtypeStruct((b, s, w), dt) for _, w, dt in outs],
        compiler_params=_cparams(("parallel", "parallel")),
        name="in_proj",
    )(x, mod, g, wmain, qg, kvg, wq, wkv, cos_t, sin_t)
    return {n: r for (n, _, _), r in zip(outs, res)}


def _fold_max(s):
    out = s[:, 0:LANES]
    for j in range(1, s.shape[1] // LANES):
        out = jnp.maximum(out, s[:, j * LANES:(j + 1) * LANES])
    return out


LOG2E = math.log2(math.e)


def _grouped_loops(n, groups, body, carry):
    pos = 0
    left = n
    for g in groups:
        cnt = left // g
        carry = lax.fori_loop(0, cnt, lambda i, cr, g=g, pos=pos: body(pos + i * g, g, cr), carry)
        pos = pos + cnt * g
        left = left - cnt * g
    return carry


def _two_pass(streams, n_past, tk, s_sc, scale=1.0, groups=(4, 2, 1), chunk_of=None):
    sc2 = scale * LOG2E

    def row0(c):
        return pl.multiple_of(c * tk, tk)

    def pass1(pos, g, mxs):
        out = []
        for i, st in enumerate(streams):
            if chunk_of is None:
                parts_src = [_dot_nt(st["q"], st["k_rows"](row0(pos), g * tk)) * sc2]
                parts = [parts_src[0][:, j * tk:(j + 1) * tk] for j in range(g)]
            else:
                parts = [_dot_nt(st["q"], st["k_rows"](row0(chunk_of(pos + j)), tk)) * sc2 for j in range(g)]
            mx = mxs[i]
            for j in range(g):
                s_sc[i, pos + j] = parts[j]
                mx = jnp.maximum(mx, _fold_max(parts[j]))
            out.append(mx)
        return tuple(out)

    own = [st["s_own"] * LOG2E for st in streams]
    mxs = _grouped_loops(n_past, groups, pass1, tuple(_fold_max(s) for s in own))
    m128 = [jnp.broadcast_to(jnp.max(mx, axis=1, keepdims=True), mx.shape) for mx in mxs]

    def p_of(s, mt):
        return jnp.exp2(s - jnp.concatenate([mt] * (s.shape[1] // LANES), axis=1)).astype(BF16)

    def pass2(pos, g, accs):
        out = []
        for i, st in enumerate(streams):
            if chunk_of is None:
                p = jnp.concatenate([p_of(s_sc[i, pos + j], m128[i]) for j in range(g)], axis=1)
                out.append(accs[i] + _dot(p, st["v_rows"](row0(pos), g * tk)))
            else:
                acc = accs[i]
                for j in range(g):
                    acc = acc + _dot(p_of(s_sc[i, pos + j], m128[i]), st["v_rows"](row0(chunk_of(pos + j)), tk))
                out.append(acc)
        return tuple(out)

    init = tuple(_dot(p_of(own[i], m128[i]), st["v_own"]) for i, st in enumerate(streams))
    return _grouped_loops(n_past, groups, pass2, init)


def _finish_ones(acc):
    return acc[:, :HEAD_DIM] * (1.0 / jnp.maximum(acc[:, HEAD_DIM:HEAD_DIM + 1], TINY))


def _dist(tq, tk, t0, k0):
    r = lax.broadcasted_iota(jnp.int32, (tq, tk), 0)
    c = lax.broadcasted_iota(jnp.int32, (tq, tk), 1)
    return (r - c) + (t0 - k0)


SWA_TQ = 256


def _swa_kernel(sink_ref, q_ref, k_ref, v_ref, o_ref):
    t0 = pl.program_id(1) * SWA_TQ
    slab = SWA_WINDOW + SWA_TQ
    k0 = pl.multiple_of(jnp.maximum(t0 - SWA_WINDOW, 0), SWA_WINDOW)
    dist = _dist(SWA_TQ, slab, t0, k0)
    ok = jnp.logical_and(dist >= 0, dist < SWA_WINDOW)
    tpos = (lax.broadcasted_iota(jnp.int32, (SWA_TQ, 1), 0) + t0).astype(F32)
    outs = []
    for hd in range(N_HEADS):
        kv = hd // (N_HEADS // SWA_KV_HEADS)
        ksl = slice(kv * LANES, (kv + 1) * LANES)
        s = jnp.where(ok, _dot_nt(q_ref[0, :, hd * LANES:(hd + 1) * LANES], k_ref[0, pl.ds(k0, slab), ksl]), NEG)
        sink = sink_ref[hd] + SLOPES_A[hd] * tpos
        m = jnp.maximum(jnp.max(s, axis=1, keepdims=True), sink)
        acc = _dot(jnp.exp(s - m).astype(BF16), v_ref[0, pl.ds(k0, slab), ksl])
        den = acc[:, HEAD_DIM:HEAD_DIM + 1] + jnp.exp(sink - m)
        outs.append(acc[:, :HEAD_DIM] * (1.0 / jnp.maximum(den, TINY)))
    o_ref[0] = jnp.concatenate(outs, axis=1).astype(o_ref.dtype)


def _swa(qA, kA, vA, sinks):
    b, s, _ = qA.shape
    assert s % SWA_TQ == 0 and s >= SWA_WINDOW + SWA_TQ
    tile = lambda w: pl.BlockSpec((1, SWA_TQ, w), lambda i, j: (i, j, 0))
    whole = lambda w: pl.BlockSpec((1, s, w), lambda i, j: (i, 0, 0))
    return pl.pallas_call(
        _swa_kernel,
        grid=(b, s // SWA_TQ),
        in_specs=[pl.BlockSpec(memory_space=pltpu.SMEM), tile(N_HEADS * LANES), whole(SWA_KV_HEADS * LANES),
                  whole(SWA_KV_HEADS * LANES)],
        out_specs=tile(N_HEADS * HEAD_DIM),
        out_shape=jax.ShapeDtypeStruct((b, s, N_HEADS * HEAD_DIM), BF16),
        compiler_params=_cparams(("parallel", "arbitrary")),
        name="swa",
    )(sinks, qA, kA, vA)


MLA_SCALE = float((MLA_NOPE + MLA_ROPE) ** -0.5)


def _rows_at(ref, sl):
    return lambda row0, n: ref[0, pl.ds(row0, n), sl]


def _mla_kernel(q_ref, k_ref, v_ref, o_ref, s_sc, *, tq, tk):
    t0 = pl.program_id(2) * tq
    n_past = t0 // tk
    k_own = pl.multiple_of(n_past * tk, tk)
    bias = jnp.where(_dist(tq, tk, t0, k_own) >= 0, 0.0, NEG)
    streams = []
    for hd in range(2):
        sl = slice(hd * LANES, (hd + 1) * LANES)
        q = q_ref[0, :, sl]
        s_own = _dot_nt(q, k_ref[0, pl.ds(k_own, tk), sl]) * MLA_SCALE + bias
        streams.append(dict(q=q, k_rows=_rows_at(k_ref, sl), v_rows=_rows_at(v_ref, sl), s_own=s_own,
                            v_own=v_ref[0, pl.ds(k_own, tk), sl]))
    accs = _two_pass(streams, n_past, tk, s_sc, scale=MLA_SCALE)
    o_ref[0] = jnp.concatenate([_finish_ones(acc) for acc in accs], axis=1).astype(o_ref.dtype)


def _mla(qB, kB, vB, tq=512, tk=512):
    b, s, _ = qB.shape
    tk = min(tk, s)
    nq = s // tq
    resident = lambda: pl.BlockSpec((1, s, 2 * LANES), lambda bi, p, i: (bi, 0, p), pipeline_mode=pl.Buffered(1))
    return pl.pallas_call(
        functools.partial(_mla_kernel, tq=tq, tk=tk),
        grid=(b, 2, nq),
        in_specs=[pl.BlockSpec((1, tq, 2 * LANES), lambda bi, p, i: (bi, i, p)), resident(), resident()],
        out_specs=pl.BlockSpec((1, tq, LANES), lambda bi, p, i: (bi, i, p)),
        out_shape=jax.ShapeDtypeStruct((b, s, N_HEADS * HEAD_DIM), BF16),
        scratch_shapes=[pltpu.VMEM((2, max(s // tk - 1, 1), tq, tk), F32)],
        compiler_params=_cparams(("parallel", "parallel", "arbitrary")),
        name="mla_attn",
    )(qB, kB, vB)


def _gelu_tanh(x):
    return x * (0.5 * (1.0 + jnp.tanh(math.sqrt(2.0 / math.pi) * (x + 0.044715 * (x * x * x)))))


def _nsa_cmp_kernel(ch_ref, w1_ref, pos_ref, w1f_ref, w2_ref, kc_ref, vc_ref):
    nc = ch_ref.shape[1]
    u = _dot(ch_ref[0], w1_ref[...])
    for idx, out in ((0, kc_ref), (1, vc_ref)):
        pos = jnp.broadcast_to(pos_ref[idx], (8, pos_ref.shape[2]))
        w1f = w1f_ref[idx]
        p_hi = pos.astype(BF16)
        p_lo = (pos - p_hi.astype(F32)).astype(BF16)
        w_hi = w1f.astype(BF16)
        w_lo = (w1f - w_hi.astype(F32)).astype(BF16)
        posb = (_dot(p_hi, w_hi) + _dot(p_hi, w_lo) + _dot(p_lo, w_hi))[0:1, :]
        top = u[:, (2 * idx) * LANES:(2 * idx + 1) * LANES]
        bot = u[:, (2 * idx + 1) * LANES:(2 * idx + 2) * LANES]
        hid = top + pltpu.roll(bot, nc - 1, 0) + posb
        res = _dot(_gelu_tanh(hid).astype(BF16), w2_ref[idx])
        if idx == 0:
            lane = lax.broadcasted_iota(jnp.int32, (nc, LANES), 1)
            cend = lax.broadcasted_iota(jnp.int32, (nc, LANES), 0) * NSA_CMP_STRIDE + (NSA_CMP_LEN - 1)
            res = res + _pos_lanes(lane, cend)
        out[0] = res.astype(out.dtype)


def _nsa_compress(ch, w1aug, pos2, w1f, w2p):
    b, nc, w = ch.shape
    full = lambda a: pl.BlockSpec(a.shape, lambda i: (0,) * a.ndim)
    return pl.pallas_call(
        _nsa_cmp_kernel,
        grid=(b,),
        in_specs=[pl.BlockSpec((1, nc, w), lambda i: (i, 0, 0)), full(w1aug), full(pos2), full(w1f), full(w2p)],
        out_specs=[pl.BlockSpec((1, nc, LANES), lambda i: (i, 0, 0))] * 2,
        out_shape=[jax.ShapeDtypeStruct((b, nc, LANES), BF16)] * 2,
        compiler_params=_cparams(("parallel",)),
        name="nsa_compress",
    )(ch, w1aug, pos2, w1f, w2p)


NSA_CHUNK = 256


def _nsa_kernel(q_ref, kc_ref, vc_ref, kv_ref, gl_ref, mm_ref, o_ref, imp_sc, s_sc, act_sm):
    n = pl.program_id(1)
    t0 = n * QB
    nc = kc_ref.shape[1]
    nbl = LANES
    qh = [q_ref[0, :, hd * LANES:(hd + 1) * LANES] for hd in range(N_HEADS)]

    def stack(x):
        return jnp.concatenate([x] * N_HEADS, axis=0)

    qs = jnp.concatenate(qh, axis=0)

    slab = NSA_WINDOW + QB
    k0w = pl.multiple_of(jnp.maximum(n - NSA_WINDOW // QB, 0) * QB, QB)
    dist = _dist(QB, slab, t0, k0w)
    rblk = 64
    n_rblk = N_HEADS * QB // rblk
    bias_w = jnp.where(jnp.logical_and(dist >= 0, dist < NSA_WINDOW), 0.0, NEG)
    s = _dot_nt(qs, kv_ref[0, pl.ds(k0w, slab), 3 * LANES:4 * LANES])
    blocks = []
    for rb in range(n_rblk):
        qo = (rb * rblk) % QB
        blk = s[rb * rblk:(rb + 1) * rblk] + bias_w[qo:qo + rblk]
        blocks.append(jnp.exp(blk - jnp.max(blk, axis=1, keepdims=True)).astype(BF16))
    o_win = _finish_ones(_dot(jnp.concatenate(blocks, axis=0), kv_ref[0, pl.ds(k0w, slab), 4 * LANES:5 * LANES]))

    tq1 = lax.broadcasted_iota(jnp.int32, (QB, 1), 0) + t0
    cidx = lax.broadcasted_iota(jnp.int32, (QB, nc), 1)
    ok = jnp.logical_and(tq1 >= cidx * NSA_CMP_STRIDE + (NSA_CMP_LEN - 1), cidx < nc - 1)
    bias_c = jnp.where(ok, 0.0, NEG)
    row_ok = jnp.where(tq1 >= NSA_CMP_LEN - 1, 1.0, 0.0)
    s = _dot_nt(qs, kc_ref[0])
    blocks = []
    psum_parts = [None] * (QB // rblk)
    for rb in range(n_rblk):
        qo = (rb * rblk) % QB
        blk = s[rb * rblk:(rb + 1) * rblk] + bias_c[qo:qo + rblk]
        e = jnp.exp(blk - jnp.max(blk, axis=1, keepdims=True))
        p = e * (row_ok[qo:qo + rblk] / jnp.maximum(jnp.sum(e, axis=1, keepdims=True), TINY))
        blocks.append(p.astype(BF16))
        part = psum_parts[qo // rblk]
        psum_parts[qo // rblk] = p if part is None else part + p
    o_cmp = _dot(jnp.concatenate(blocks, axis=0), vc_ref[0])
    psum = jnp.concatenate(psum_parts, axis=0)

    mm = mm_ref[...]
    imp = None
    for piece in _split3(psum):
        d = _dot(piece, mm)
        imp = d if imp is None else imp + d
    imp_t = imp.T
    jrow = lax.broadcasted_iota(jnp.int32, (nbl, QB), 0)
    tcol = lax.broadcasted_iota(jnp.int32, (nbl, QB), 1) + t0
    cur = tcol >> 6
    cand = jrow <= cur
    forced = jnp.logical_or(jnp.logical_or(jrow == 0, jrow == cur), jrow == cur - 1)
    a = jnp.where(cand, jnp.where(forced, BIG, imp_t), NEG)
    key = lax.bitcast_convert_type(a, jnp.int32)
    key1 = key + 1
    imp_sc[...] = key
    n_cand = (t0 + QB - 1) // NSA_SEL_BLOCK + 1

    def rank_body(i2, cnt):
        for u in range(2):
            i = 2 * i2 + u
            row = imp_sc[pl.ds(i, 1), :]
            cnt = cnt + jnp.where(row >= jnp.where(jrow <= i, key1, key), 1.0, 0.0)
        return cnt

    rank = lax.fori_loop(0, n_cand // 2, rank_body, jnp.zeros((nbl, QB), F32))
    sel_t = jnp.where(jnp.logical_and(rank < NSA_TOPN, cand), 0.0, NEG)
    selneg = sel_t.T.astype(BF16)

    q_sel = jnp.concatenate([jnp.concatenate([qh[hd], selneg], axis=1) for hd in range(N_HEADS)], axis=0)
    own = t0 // NSA_CHUNK
    k_own = pl.multiple_of(own * NSA_CHUNK, NSA_CHUNK)
    ksl, vsl = slice(0, 2 * LANES), slice(2 * LANES, 3 * LANES)
    s_own = (_dot_nt(q_sel, kv_ref[0, pl.ds(k_own, NSA_CHUNK), ksl])
             + stack(jnp.where(_dist(QB, NSA_CHUNK, t0, k_own) >= 0, 0.0, NEG)))
    n_chunks = kv_ref.shape[1] // NSA_CHUNK
    blk_shift = (NSA_CHUNK // NSA_SEL_BLOCK).bit_length() - 1
    grp = jnp.where((lax.broadcasted_iota(jnp.int32, (n_chunks, nbl), 1) >> blk_shift)
                    == lax.broadcasted_iota(jnp.int32, (n_chunks, nbl), 0), 1.0, 0.0).astype(BF16)
    picked = jnp.where(sel_t == 0.0, 1.0, 0.0).astype(BF16)
    cnt = _dot(_dot(grp, picked).astype(BF16), jnp.ones((QB, LANES), BF16))
    n_act = jnp.int32(0)
    for c in range(n_chunks):
        act_sm[n_act] = c
        n_act = n_act + jnp.where(jnp.logical_and(cnt[c, 0] > 0.0, c < own), 1, 0)
    stream = dict(q=q_sel, k_rows=_rows_at(kv_ref, ksl), v_rows=_rows_at(kv_ref, vsl),
                  s_own=s_own, v_own=kv_ref[0, pl.ds(k_own, NSA_CHUNK), vsl])
    o_sel = _finish_ones(_two_pass([stream], n_act, NSA_CHUNK, s_sc, groups=(4, 2, 1),
                                   chunk_of=lambda pos: act_sm[pos])[0])

    g = jax.nn.sigmoid(gl_ref[0])
    outs = []
    for hd in range(N_HEADS):
        rs = slice(hd * QB, (hd + 1) * QB)
        outs.append(g[:, hd:hd + 1] * o_cmp[rs, :HEAD_DIM] + g[:, N_HEADS + hd:N_HEADS + hd + 1] * o_sel[rs]
                    + g[:, 2 * N_HEADS + hd:2 * N_HEADS + hd + 1] * o_win[rs])
    o_ref[0] = jnp.concatenate(outs, axis=1).astype(o_ref.dtype)


def _nsa(qC, kcmp, vcmp, kvC, gl, m_mat):
    b, s, _ = qC.shape
    nq = s // QB
    nc = kcmp.shape[1]
    assert s >= NSA_WINDOW + QB and s % NSA_CHUNK == 0
    return pl.pallas_call(
        _nsa_kernel,
        grid=(b, nq),
        in_specs=[pl.BlockSpec((1, QB, N_HEADS * LANES), lambda i, j: (i, j, 0)),
                  pl.BlockSpec((1, nc, LANES), lambda i, j: (i, 0, 0)),
                  pl.BlockSpec((1, nc, LANES), lambda i, j: (i, 0, 0)),
                  pl.BlockSpec((1, s, 5 * LANES), lambda i, j: (i, 0, 0), pipeline_mode=pl.Buffered(1)),
                  pl.BlockSpec((1, QB, LANES), lambda i, j: (i, j, 0)),
                  pl.BlockSpec(m_mat.shape, lambda i, j: (0, 0))],
        out_specs=pl.BlockSpec((1, QB, N_HEADS * HEAD_DIM), lambda i, j: (i, j, 0)),
        out_shape=jax.ShapeDtypeStruct((b, s, N_HEADS * HEAD_DIM), BF16),
        scratch_shapes=[pltpu.VMEM((LANES, QB), jnp.int32),
                        pltpu.VMEM((1, max(s // NSA_CHUNK - 1, 1), N_HEADS * QB, NSA_CHUNK), F32),
                        pltpu.SMEM((s // NSA_CHUNK + 1,), jnp.int32)],
        compiler_params=_cparams(("parallel", "arbitrary")),
        name="nsa_attn",
    )(qC, kcmp, vcmp, kvC, gl, m_mat)


MOBA_NBR = LANES - MOBA_LANE0


MOBA_TQ = 2 * MOBA_BLOCK


def _moba_kernel(q_ref, k_ref, v_ref, o_ref, kmean_sc, gate_sc, s_sc):
    tq = MOBA_TQ
    n = pl.program_id(2)
    t0 = n * tq
    n_blk = k_ref.shape[1] // MOBA_BLOCK
    lane = lax.broadcasted_iota(jnp.int32, (1, LANES), 1)

    @pl.when(n == 0)
    def _():
        kmean_sc[...] = jnp.zeros(kmean_sc.shape, F32)
        for hd in range(2):
            for jb in range(n_blk):
                blk = k_ref[0, jb * MOBA_BLOCK:(jb + 1) * MOBA_BLOCK, hd * LANES:(hd + 1) * LANES].astype(F32)
                kmean_sc[hd, jb:jb + 1, :] = jnp.where(lane < HEAD_DIM, jnp.mean(blk, axis=0, keepdims=True), 0.0)

    jrow = lax.broadcasted_iota(jnp.int32, (MOBA_NBR, tq), 0)
    qcol = lax.broadcasted_iota(jnp.int32, (MOBA_NBR, tq), 1)
    cur = 2 * n + (qcol >> MOBA_SHIFT)
    qs = [q_ref[0, :, hd * LANES:(hd + 1) * LANES] for hd in range(2)]
    gates = []
    for hd in range(2):
        gate_t = _dot_nt(kmean_sc[hd].astype(BF16), qs[hd])[0:MOBA_NBR, :]
        a = jnp.where(jrow < cur, gate_t, NEG)
        gate_sc[hd] = a
        gates.append(a)

    def rank_body(i, cnts):
        out = []
        for hd in range(2):
            row = gate_sc[hd, pl.ds(i, 1), :]
            c_ge = jnp.where(row >= gates[hd], 1.0, 0.0)
            c_gt = jnp.where(row > gates[hd], 1.0, 0.0)
            out.append(cnts[hd] + jnp.where(jrow > i, c_ge, c_gt))
        return tuple(out)

    ranks = lax.fori_loop(0, 2 * n + 1, rank_body, tuple(jnp.zeros((MOBA_NBR, tq), F32) for _ in range(2)))
    tk = MOBA_TQ
    k_own = pl.multiple_of(t0, tk)
    bias = jnp.where(_dist(tq, tk, 0, 0) >= 0, 0.0, NEG)
    streams = []
    for hd in range(2):
        sl = slice(hd * LANES, (hd + 1) * LANES)
        keep = jnp.logical_or(jnp.logical_and(ranks[hd] < MOBA_TOPK, jrow < cur), jrow == cur)
        full_t = jnp.concatenate([jnp.zeros((MOBA_LANE0, tq), F32), jnp.where(keep, 0.0, NEG)], axis=0)
        q_sel = (qs[hd].astype(F32) + full_t.T).astype(BF16)
        s_own = _dot_nt(q_sel, k_ref[0, pl.ds(k_own, tk), sl]) + bias
        streams.append(dict(q=q_sel, k_rows=_rows_at(k_ref, sl), v_rows=_rows_at(v_ref, sl), s_own=s_own,
                            v_own=v_ref[0, pl.ds(k_own, tk), sl]))
    accs = _two_pass(streams, n, tk, s_sc)
    o_ref[0] = jnp.concatenate([_finish_ones(acc) for acc in accs], axis=1).astype(o_ref.dtype)


def _moba(qD, kD, vD):
    b, s, _ = qD.shape
    tq = MOBA_TQ
    assert s % tq == 0 and s // MOBA_BLOCK <= MOBA_NBR
    resident = lambda: pl.BlockSpec((1, s, 2 * LANES), lambda i, p, j: (i, 0, p), pipeline_mode=pl.Buffered(1))
    return pl.pallas_call(
        _moba_kernel,
        grid=(b, 2, s // tq),
        in_specs=[pl.BlockSpec((1, tq, 2 * LANES), lambda i, p, j: (i, j, p)), resident(), resident()],
        out_specs=pl.BlockSpec((1, tq, LANES), lambda i, p, j: (i, j, p)),
        out_shape=jax.ShapeDtypeStruct((b, s, N_HEADS * HEAD_DIM), BF16),
        scratch_shapes=[pltpu.VMEM((2, LANES, LANES), F32), pltpu.VMEM((2, MOBA_NBR, tq), F32),
                        pltpu.VMEM((2, max(s // tq - 1, 1), tq, tq), F32)],
        compiler_params=_cparams(("parallel", "parallel", "arbitrary")),
        name="moba_attn",
    )(qD, kD, vD)


def _post_kernel(oa_ref, ob_ref, oc_ref, od_ref, x_ref, mod_ref, wo_ref, g_ref, wu_ref, wd_ref, fg_ref, out_ref,
                 *, final, ffc):
    x = x_ref[0]
    mix = None
    for idx, ref in enumerate((oa_ref, ob_ref, oc_ref, od_ref)):
        d = _dot(ref[0], wo_ref[idx])
        mix = d if mix is None else mix + d
    x1 = x + mod_ref[0, 2:3, :] * mix
    y = x1 * lax.rsqrt(jnp.mean(x1 * x1, axis=-1, keepdims=True) + RMS_EPS) * g_ref[...]
    hb = (y * (1.0 + mod_ref[0, 4:5, :]) + mod_ref[0, 3:4, :]).astype(BF16)
    acc = None
    for c in range(wu_ref.shape[1] // ffc):
        hid = jnp.maximum(_dot(hb, wu_ref[:, c * ffc:(c + 1) * ffc]), 0.0)
        d = _dot((hid * hid).astype(BF16), wd_ref[c * ffc:(c + 1) * ffc, :])
        acc = d if acc is None else acc + d
    x2 = x1 + mod_ref[0, 5:6, :] * acc
    if final:
        x2 = x2 * lax.rsqrt(jnp.mean(x2 * x2, axis=-1, keepdims=True) + RMS_EPS) * fg_ref[...]
    out_ref[0] = x2


def _post(oa, ob, oc, od, x, mod, wo4, g, wu, wd, fg, final, ts=256, ffc=1024):
    b, s, d = x.shape
    nt = s // ts
    tok = lambda w: pl.BlockSpec((1, ts, w), lambda i, j: (i, j, 0))
    full = lambda a: pl.BlockSpec(a.shape, lambda i, j: (0,) * a.ndim, pipeline_mode=pl.Buffered(1))
    hw = N_HEADS * HEAD_DIM
    return pl.pallas_call(
        functools.partial(_post_kernel, final=final, ffc=ffc),
        grid=(b, nt),
        in_specs=[tok(hw), tok(hw), tok(hw), tok(hw), tok(d), pl.BlockSpec((1, N_ADA, d), lambda i, j: (i, 0, 0)),
                  full(wo4), full(g), full(wu), full(wd), full(fg)],
        out_specs=tok(d),
        out_shape=jax.ShapeDtypeStruct((b, s, d), F32),
        compiler_params=_cparams(("parallel", "parallel")),
        name="post_mlp",
    )(oa, ob, oc, od, x, mod, wo4, g, wu, wd, fg)


def _rot_half_cols(w):
    half = w.shape[1] // 2
    return jnp.concatenate([-w[:, half:], w[:, :half]], axis=1)


def _layer_weights(w_in, mla_qg, mla_kvg, w_uq, w_ukv, pos_k, pos_v, ck_w1, ck_w2, cv_w1, cv_w2):
    d = w_in.shape[0]
    z = lambda n: jnp.zeros((d, n), F32)
    o = 0

    def take(n):
        nonlocal o
        r = w_in[:, o:o + n]
        o += n
        return r

    qa, ka, va = take(256), take(128), take(128)
    cq, ckv, kpe = take(MLA_Q_RANK), take(MLA_KV_RANK), take(MLA_ROPE)
    qc, kc, vc, ks, vs, kw, vw, gl = take(256), take(64), take(64), take(64), take(64), take(64), take(64), take(12)
    qd, kd, vd = take(256), take(256), take(256)
    sc = 1.0 / 8.0
    kpe_main = jnp.concatenate([z(64), kpe, z(32)], axis=1)
    kpe_rot = jnp.concatenate([z(64), _rot_half_cols(kpe), z(32)], axis=1)
    p64 = lambda w: jnp.pad(w, ((0, 0), (0, LANES - w.shape[1])))
    groups = {
        "mla": jnp.concatenate([cq, z(64), ckv, kpe_main, kpe_rot], axis=1),
        "A": jnp.concatenate([qa * sc, ka, va], axis=1),
        "C": jnp.concatenate([qc * sc, kc, vc, ks, vs, kw, vw, p64(gl)], axis=1),
        "D": jnp.concatenate([qd * sc, kd, vd], axis=1),
    }
    wmain = jnp.concatenate([groups[n] for n, _ in _IN_GROUPS], axis=1).astype(BF16)

    dq = MLA_NOPE + MLA_ROPE
    wq_cols, wkv_cols = [], []
    for hd in range(N_HEADS):
        wh = w_uq[:, hd * dq:(hd + 1) * dq]
        zq = lambda n: jnp.zeros((MLA_Q_RANK, n), F32)
        wq_cols += [wh, zq(32), zq(64), _rot_half_cols(wh[:, MLA_NOPE:]), zq(32)]
        wk = w_ukv[:, hd * 128:(hd + 1) * 128]
        zk = jnp.zeros((MLA_KV_RANK, 64), F32)
        wkv_cols += [wk[:, :64], zk, wk[:, 64:], zk]
    wq = jnp.pad(jnp.concatenate(wq_cols, axis=1), ((0, 256 - MLA_Q_RANK), (0, 0))).astype(BF16)
    wkv = jnp.concatenate(wkv_cols, axis=1).astype(BF16)
    qg = jnp.pad(mla_qg, (0, 256 - MLA_Q_RANK)).reshape(1, 256)
    kvg = mla_kvg.reshape(1, MLA_KV_RANK)


    def aug(w1, is_v):
        w = w1.reshape(NSA_CMP_LEN, HEAD_DIM, NSA_CMP_HIDDEN)
        zz = jnp.zeros_like(w)
        w = jnp.concatenate([zz, w] if is_v else [w, zz], axis=1)
        w = w.reshape(NSA_CMP_LEN * LANES, NSA_CMP_HIDDEN)
        return w[:NSA_CMP_STRIDE * LANES], w[NSA_CMP_STRIDE * LANES:]

    kt, kb = aug(ck_w1, False)
    vt, vb = aug(cv_w1, True)
    w1aug = jnp.concatenate([kt, kb, vt, vb], axis=1).astype(BF16)
    pos2 = jnp.stack([pos_k.reshape(1, -1), pos_v.reshape(1, -1)])
    w1f = jnp.stack([ck_w1, cv_w1])
    w2p = jnp.stack([p64(ck_w2), p64(cv_w2)]).astype(BF16)
    return wmain, qg, kvg, wq, wkv, w1aug, pos2, w1f, w2p


def _rope_tables(s):
    half = MLA_ROPE // 2
    freqs = ROPE_THETA ** (-jnp.arange(half, dtype=F32) / half)
    ang = jnp.arange(s, dtype=F32)[:, None] * freqs[None, :]
    cos, sin = jnp.cos(ang), jnp.sin(ang)
    ones, zeros = jnp.ones((s, MLA_NOPE), F32), jnp.zeros((s, MLA_NOPE), F32)
    tail = jnp.zeros((s, LANES - MLA_NOPE - MLA_ROPE), F32)
    return (jnp.concatenate([ones, cos, cos, tail], axis=1), jnp.concatenate([zeros, sin, sin, tail], axis=1))


def _nsa_imp_matrix(s):
    n_sel = s // NSA_SEL_BLOCK
    nc = s // NSA_CMP_STRIDE
    assert n_sel <= LANES
    c = np.arange(nc)[:, None]
    j = np.arange(LANES)[None, :]
    m = ((c >= 4 * j - 1) & (c <= 4 * j + 3) & (c < nc - 1) & (j < n_sel)).astype(np.float32)
    return jnp.asarray(m, BF16)


def kernel(x, c, norm_mix_g, norm_mlp_g, w_ada, b_ada, w_in, w_out, swa_sinks, mla_q_norm_g, mla_kv_norm_g,
           mla_w_uq, mla_w_ukv, nsa_cmp_pos_k, nsa_cmp_pos_v, nsa_cmp_k_w1, nsa_cmp_k_w2, nsa_cmp_v_w1,
           nsa_cmp_v_w2, w_up, w_down, final_norm_g):
    b, s, d = x.shape
    depth = w_in.shape[0]
    assert b <= SUBLANES
    c8 = jnp.pad(c, ((0, SUBLANES - b), (0, 0)))
    mod_all = _ada(c8, w_ada, b_ada)
    cos_t, sin_t = _rope_tables(s)
    m_mat = _nsa_imp_matrix(s)
    fg = final_norm_g.reshape(1, d)
    for l in range(depth):
        mod = mod_all[l, :b].reshape(b, N_ADA, d)
        wmain, qg, kvg, wq, wkv, w1aug, pos2, w1f, w2p = _layer_weights(
            w_in[l], mla_q_norm_g[l], mla_kv_norm_g[l], mla_w_uq[l], mla_w_ukv[l], nsa_cmp_pos_k[l],
            nsa_cmp_pos_v[l], nsa_cmp_k_w1[l], nsa_cmp_k_w2[l], nsa_cmp_v_w1[l], nsa_cmp_v_w2[l])
        z = _in_proj(x, mod, norm_mix_g[l].reshape(1, d), wmain, qg, kvg, wq, wkv, cos_t, sin_t)
        o_a = _swa(z["qA"], z["kA"], z["vA"], swa_sinks[l])
        o_b = _mla(z["qB"], z["kB"], z["vB"])
        ch = z["kcvc"].reshape(b, s // NSA_CMP_STRIDE, NSA_CMP_STRIDE * LANES)
        kcmp, vcmp = _nsa_compress(ch, w1aug, pos2, w1f, w2p)
        o_c = _nsa(z["qC"], kcmp, vcmp, z["kvC"], z["gl"], m_mat)
        o_d = _moba(z["qD"], z["kD"], z["vD"])
        wo4 = w_out[l].astype(BF16).reshape(4, N_HEADS * HEAD_DIM, d)
        x = _post(o_a, o_b, o_c, o_d, x, mod, wo4, norm_mlp_g[l].reshape(1, d), w_up[l].astype(BF16),
                  w_down[l].astype(BF16), fg, final=(l == depth - 1))
    return x
```

```python
import functools
import math

import numpy as np
import jax
import jax.numpy as jnp
from jax import lax
from jax.experimental import pallas as pl
from jax.experimental.pallas import tpu as pltpu

F32 = jnp.float32
BF16 = jnp.bfloat16

HEAD_DIM = 64
LANES = 128
SUBLANES = 8
QB = 256
NEG = -1e30
TINY = 1e-30
BIG = 1e9
RMS_EPS = 1e-6
N_ADA = 6
N_HEADS = 4

SWA_KV_HEADS = 2
SWA_WINDOW = 128
MLA_Q_RANK = 192
MLA_KV_RANK = 128
MLA_NOPE = 64
MLA_ROPE = 32
ROPE_THETA = 10000.0
NSA_CMP_LEN = 32
NSA_CMP_STRIDE = 16
NSA_CMP_HIDDEN = 128
NSA_SEL_BLOCK = 64
NSA_SEL_SHIFT = 6
NSA_TOPN = 16
NSA_WINDOW = 512
MOBA_BLOCK = 256
MOBA_SHIFT = 8
MOBA_TOPK = 3
MOBA_LANE0 = 96
N_ALIBI = 3 * N_HEADS

_S_ALL = [2.0 ** (-8.0 * (i + 1) / N_ALIBI) for i in range(N_ALIBI)]
SLOPES_A = [float(np.float32(v)) for v in _S_ALL[0::3]]
SLOPES_C = [float(np.float32(v)) for v in _S_ALL[1::3]]
SLOPES_D = [float(np.float32(v)) for v in _S_ALL[2::3]]

VMEM_LIMIT = 56 * 1024 * 1024

_NT = (((1,), (1,)), ((), ()))


def _cparams(sem):
    return pltpu.CompilerParams(dimension_semantics=sem, vmem_limit_bytes=VMEM_LIMIT)


def _split3(a):
    hi = a.astype(BF16)
    r1 = a - hi.astype(F32)
    mid = r1.astype(BF16)
    lo = (r1 - mid.astype(F32)).astype(BF16)
    return hi, mid, lo


def _dot(a, b):
    return jnp.dot(a, b, preferred_element_type=F32)


def _dot_nt(a, b):
    return lax.dot_general(a, b, _NT, preferred_element_type=F32)


def _split3_const(x):
    x = np.float32(x)
    hi = np.float32(np.asarray(x, dtype=BF16))
    mid = np.float32(np.asarray(np.float32(x - hi), dtype=BF16))
    lo = np.float32(np.asarray(np.float32(x - hi - mid), dtype=BF16))
    return float(hi), float(mid), float(lo)


def _pieces_lanes(lane, pieces, repeat):
    out = jnp.zeros(lane.shape, F32)
    for i, pc in enumerate(pieces):
        lo = HEAD_DIM + i * repeat
        out = jnp.where(jnp.logical_and(lane >= lo, lane < lo + repeat), pc, out)
    return out


def _pos_lanes(lane, pos):
    hi = ((pos >> 7) << 7).astype(F32)
    lo = (pos & 127).astype(F32)
    k = lane - HEAD_DIM
    return jnp.where(jnp.logical_and(k >= 0, k < 6), jnp.where((k & 1) == 0, hi, lo), 0.0)


def _ada_kernel(c_ref, w_ref, b_ref, o_ref):
    c = c_ref[...]
    a = c * jax.nn.sigmoid(c)
    w = w_ref[0]
    a_hi = a.astype(BF16)
    a_lo = (a - a_hi.astype(F32)).astype(BF16)
    w_hi = w.astype(BF16)
    w_lo = (w - w_hi.astype(F32)).astype(BF16)
    o_ref[0] = _dot(a_hi, w_hi) + _dot(a_hi, w_lo) + _dot(a_lo, w_hi) + b_ref[0]


def _ada(c8, w_ada, b_ada):
    depth, d, n = w_ada.shape
    nb = n // d
    return pl.pallas_call(
        _ada_kernel,
        grid=(depth, nb),
        in_specs=[
            pl.BlockSpec((SUBLANES, d), lambda l, j: (0, 0)),
            pl.BlockSpec((1, d, d), lambda l, j: (l, 0, j)),
            pl.BlockSpec((1, 1, d), lambda l, j: (l, 0, j)),
        ],
        out_specs=pl.BlockSpec((1, SUBLANES, d), lambda l, j: (l, 0, j)),
        out_shape=jax.ShapeDtypeStruct((depth, SUBLANES, n), F32),
        compiler_params=_cparams(("parallel", "parallel")),
        name="ada_mod",
    )(c8, w_ada, b_ada.reshape(depth, 1, n))


_IN_GROUPS = (("mla", 640), ("A", 512), ("C", 768), ("D", 768))
_IN_OFFS = {}
_o = 0
for _n, _w in _IN_GROUPS:
    _IN_OFFS[_n] = (_o, _o + _w)
    _o += _w


def _in_proj_kernel(x_ref, mod_ref, g_ref, w_ref, qg_ref, kvg_ref, wq_ref, wkv_ref, cos_ref, sin_ref,
                    qB_ref, kB_ref, vB_ref, qA_ref, kA_ref, vA_ref, qC_ref, kcvc_ref, kvC_ref, gl_ref,
                    qD_ref, kD_ref, vD_ref):
    x = x_ref[0]
    y = x * lax.rsqrt(jnp.mean(x * x, axis=-1, keepdims=True) + RMS_EPS) * g_ref[...]
    h = y * (1.0 + mod_ref[0, 1:2, :]) + mod_ref[0, 0:1, :]
    hb = h.astype(BF16)

    def proj(name):
        a, b = _IN_OFFS[name]
        return _dot(hb, w_ref[:, a:b])

    ts = x.shape[0]
    lane = lax.broadcasted_iota(jnp.int32, (ts, LANES), 1)
    pos = lax.broadcasted_iota(jnp.int32, (ts, LANES), 0) + pl.program_id(1) * ts
    low = lane < HEAD_DIM
    one64 = jnp.where(lane == HEAD_DIM, 1.0, 0.0)
    ones3 = jnp.where(jnp.logical_and(lane >= HEAD_DIM, lane < HEAD_DIM + 3), 1.0, 0.0)
    blk_hot = jnp.where(lane - MOBA_LANE0 == (pos >> MOBA_SHIFT), 1.0, 0.0)
    posf = pos.astype(F32)
    pos_lanes = _pos_lanes(lane, pos)

    def head_tile(z, hd, extra):
        pair = z[:, (hd // 2) * LANES:(hd // 2 + 1) * LANES]
        if hd % 2:
            pair = pltpu.roll(pair, HEAD_DIM, 1)
        return jnp.where(low, pair, extra).astype(BF16)

    za = proj("A")
    for hd in range(N_HEADS):
        qA_ref[0, :, hd * LANES:(hd + 1) * LANES] = head_tile(
            za, hd, _pieces_lanes(lane, _split3_const(SLOPES_A[hd]), repeat=2))
    for kv in range(SWA_KV_HEADS):
        sl = slice(kv * LANES, (kv + 1) * LANES)
        kA_ref[0, :, sl] = head_tile(za, N_HEADS + kv, pos_lanes)
        vA_ref[0, :, sl] = head_tile(za, N_HEADS + SWA_KV_HEADS + kv, one64)

    zc = proj("C")
    for hd in range(N_HEADS):
        qC_ref[0, :, hd * LANES:(hd + 1) * LANES] = head_tile(
            zc, hd, _pieces_lanes(lane, _split3_const(SLOPES_C[hd]), repeat=2))
    kcvc_ref[0] = zc[:, 2 * LANES:3 * LANES].astype(BF16)
    kvC_ref[0, :, 0:LANES] = head_tile(zc, 6, pos_lanes)
    kvC_ref[0, :, LANES:2 * LANES] = jnp.where(lane == (pos >> NSA_SEL_SHIFT), 1.0, 0.0).astype(BF16)
    kvC_ref[0, :, 2 * LANES:3 * LANES] = head_tile(zc, 7, one64)
    kvC_ref[0, :, 3 * LANES:4 * LANES] = head_tile(zc, 8, pos_lanes)
    kvC_ref[0, :, 4 * LANES:5 * LANES] = head_tile(zc, 9, one64)
    gl_ref[0] = zc[:, 5 * LANES:6 * LANES]

    zd = proj("D")
    for hd in range(N_HEADS):
        sl = slice(hd * LANES, (hd + 1) * LANES)
        hi, mid, lo = _split3(posf * SLOPES_D[hd])
        alibi = jnp.where(lane == HEAD_DIM, hi.astype(F32),
                          jnp.where(lane == HEAD_DIM + 1, mid.astype(F32),
                                    jnp.where(lane == HEAD_DIM + 2, lo.astype(F32), 0.0)))
        qD_ref[0, :, sl] = head_tile(zd, hd, ones3)
        kD_ref[0, :, sl] = head_tile(zd, N_HEADS + hd, alibi + blk_hot)
        vD_ref[0, :, sl] = head_tile(zd, 2 * N_HEADS + hd, one64)

    zb = proj("mla")
    cq = zb[:, 0:256]
    ckv = zb[:, 256:384]
    kpm = zb[:, 384:512]
    kpr = zb[:, 512:640]
    cqn = cq * lax.rsqrt(jnp.sum(cq * cq, axis=-1, keepdims=True) * (1.0 / MLA_Q_RANK) + RMS_EPS) * qg_ref[...]
    ckvn = ckv * lax.rsqrt(jnp.mean(ckv * ckv, axis=-1, keepdims=True) + RMS_EPS) * kvg_ref[...]
    qall = _dot(cqn.astype(BF16), wq_ref[...])
    kvall = _dot(ckvn.astype(BF16), wkv_ref[...])
    cosq = cos_ref[...]
    sinq = sin_ref[...]
    kpe = kpm * cosq + kpr * sinq
    for hd in range(N_HEADS):
        lo = hd * 2 * LANES
        sl = slice(hd * LANES, (hd + 1) * LANES)
        qB_ref[0, :, sl] = (qall[:, lo:lo + LANES] * cosq + qall[:, lo + LANES:lo + 2 * LANES] * sinq).astype(BF16)
        kB_ref[0, :, sl] = (kvall[:, lo:lo + LANES] + kpe).astype(BF16)
        vB_ref[0, :, sl] = (kvall[:, lo + LANES:lo + 2 * LANES] + one64).astype(BF16)


def _in_proj(x, mod, g, wmain, qg, kvg, wq, wkv, cos_t, sin_t, ts=512):
    b, s, d = x.shape
    nt = s // ts
    tok = lambda w: pl.BlockSpec((1, ts, w), lambda i, j: (i, j, 0))
    full = lambda a: pl.BlockSpec(a.shape, lambda i, j: (0,) * a.ndim, pipeline_mode=pl.Buffered(1))
    outs = [("qB", 512, BF16), ("kB", 512, BF16), ("vB", 512, BF16), ("qA", 512, BF16), ("kA", 256, BF16),
            ("vA", 256, BF16), ("qC", 512, BF16), ("kcvc", 128, BF16), ("kvC", 640, BF16), ("gl", 128, F32),
            ("qD", 512, BF16), ("kD", 512, BF16), ("vD", 512, BF16)]
    res = pl.pallas_call(
        _in_proj_kernel,
        grid=(b, nt),
        in_specs=[tok(d), pl.BlockSpec((1, N_ADA, d), lambda i, j: (i, 0, 0)), full(g), full(wmain), full(qg),
                  full(kvg), full(wq), full(wkv),
                  pl.BlockSpec((ts, LANES), lambda i, j: (j, 0)), pl.BlockSpec((ts, LANES), lambda i, j: (j, 0))],
        out_specs=[tok(w) for _, w, _ in outs],
        out_shape=[jax.ShapeDtypeStruct((b, s, w), dt) for _, w, dt in outs],
        compiler_params=_cparams(("parallel", "parallel")),
        name="in_proj",
    )(x, mod, g, wmain, qg, kvg, wq, wkv, cos_t, sin_t)
    return {n: r for (n, _, _), r in zip(outs, res)}


def _fold_max(s):
    out = s[:, 0:LANES]
    for j in range(1, s.shape[1] // LANES):
        out = jnp.maximum(out, s[:, j * LANES:(j + 1) * LANES])
    return out


LOG2E = math.log2(math.e)


def _grouped_loops(n, groups, body, carry):
    pos = 0
    left = n
    for g in groups:
        cnt = left // g
        carry = lax.fori_loop(0, cnt, lambda i, cr, g=g, pos=pos: body(pos + i * g, g, cr), carry)
        pos = pos + cnt * g
        left = left - cnt * g
    return carry


def _two_pass(streams, n_past, tk, s_sc, scale=1.0, groups=(4, 2, 1), chunk_of=None):
    sc2 = scale * LOG2E

    def row0(c):
        return pl.multiple_of(c * tk, tk)

    def pass1(pos, g, mxs):
        out = []
        for i, st in enumerate(streams):
            if chunk_of is None:
                parts_src = [_dot_nt(st["q"], st["k_rows"](row0(pos), g * tk)) * sc2]
                parts = [parts_src[0][:, j * tk:(j + 1) * tk] for j in range(g)]
            else:
                parts = [_dot_nt(st["q"], st["k_rows"](row0(chunk_of(pos + j)), tk)) * sc2 for j in range(g)]
            mx = mxs[i]
            for j in range(g):
                s_sc[i, pos + j] = parts[j]
                mx = jnp.maximum(mx, _fold_max(parts[j]))
            out.append(mx)
        return tuple(out)

    own = [st["s_own"] * LOG2E for st in streams]
    mxs = _grouped_loops(n_past, groups, pass1, tuple(_fold_max(s) for s in own))
    m128 = [jnp.broadcast_to(jnp.max(mx, axis=1, keepdims=True), mx.shape) for mx in mxs]

    def p_of(s, mt):
        return jnp.exp2(s - jnp.concatenate([mt] * (s.shape[1] // LANES), axis=1)).astype(BF16)

    def pass2(pos, g, accs):
        out = []
        for i, st in enumerate(streams):
            if chunk_of is None:
                p = jnp.concatenate([p_of(s_sc[i, pos + j], m128[i]) for j in range(g)], axis=1)
                out.append(accs[i] + _dot(p, st["v_rows"](row0(pos), g * tk)))
            else:
                acc = accs[i]
                for j in range(g):
                    acc = acc + _dot(p_of(s_sc[i, pos + j], m128[i]), st["v_rows"](row0(chunk_of(pos + j)), tk))
                out.append(acc)
        return tuple(out)

    init = tuple(_dot(p_of(own[i], m128[i]), st["v_own"]) for i, st in enumerate(streams))
    return _grouped_loops(n_past, groups, pass2, init)


def _finish_ones(acc):
    return acc[:, :HEAD_DIM] * (1.0 / jnp.maximum(acc[:, HEAD_DIM:HEAD_DIM + 1], TINY))


def _dist(tq, tk, t0, k0):
    r = lax.broadcasted_iota(jnp.int32, (tq, tk), 0)
    c = lax.broadcasted_iota(jnp.int32, (tq, tk), 1)
    return (r - c) + (t0 - k0)


SWA_TQ = 256


def _swa_kernel(sink_ref, q_ref, k_ref, v_ref, o_ref):
    t0 = pl.program_id(1) * SWA_TQ
    slab = SWA_WINDOW + SWA_TQ
    k0 = pl.multiple_of(jnp.maximum(t0 - SWA_WINDOW, 0), SWA_WINDOW)
    dist = _dist(SWA_TQ, slab, t0, k0)
    ok = jnp.logical_and(dist >= 0, dist < SWA_WINDOW)
    tpos = (lax.broadcasted_iota(jnp.int32, (SWA_TQ, 1), 0) + t0).astype(F32)
    outs = []
    for hd in range(N_HEADS):
        kv = hd // (N_HEADS // SWA_KV_HEADS)
        ksl = slice(kv * LANES, (kv + 1) * LANES)
        s = jnp.where(ok, _dot_nt(q_ref[0, :, hd * LANES:(hd + 1) * LANES], k_ref[0, pl.ds(k0, slab), ksl]), NEG)
        sink = sink_ref[hd] + SLOPES_A[hd] * tpos
        m = jnp.maximum(jnp.max(s, axis=1, keepdims=True), sink)
        acc = _dot(jnp.exp(s - m).astype(BF16), v_ref[0, pl.ds(k0, slab), ksl])
        den = acc[:, HEAD_DIM:HEAD_DIM + 1] + jnp.exp(sink - m)
        outs.append(acc[:, :HEAD_DIM] * (1.0 / jnp.maximum(den, TINY)))
    o_ref[0] = jnp.concatenate(outs, axis=1).astype(o_ref.dtype)


def _swa(qA, kA, vA, sinks):
    b, s, _ = qA.shape
    assert s % SWA_TQ == 0 and s >= SWA_WINDOW + SWA_TQ
    tile = lambda w: pl.BlockSpec((1, SWA_TQ, w), lambda i, j: (i, j, 0))
    whole = lambda w: pl.BlockSpec((1, s, w), lambda i, j: (i, 0, 0))
    return pl.pallas_call(
        _swa_kernel,
        grid=(b, s // SWA_TQ),
        in_specs=[pl.BlockSpec(memory_space=pltpu.SMEM), tile(N_HEADS * LANES), whole(SWA_KV_HEADS * LANES),
                  whole(SWA_KV_HEADS * LANES)],
        out_specs=tile(N_HEADS * HEAD_DIM),
        out_shape=jax.ShapeDtypeStruct((b, s, N_HEADS * HEAD_DIM), BF16),
        compiler_params=_cparams(("parallel", "arbitrary")),
        name="swa",
    )(sinks, qA, kA, vA)


MLA_SCALE = float((MLA_NOPE + MLA_ROPE) ** -0.5)


def _rows_at(ref, sl):
    return lambda row0, n: ref[0, pl.ds(row0, n), sl]


def _mla_kernel(q_ref, k_ref, v_ref, o_ref, s_sc, *, tq, tk):
    t0 = pl.program_id(2) * tq
    n_past = t0 // tk
    k_own = pl.multiple_of(n_past * tk, tk)
    bias = jnp.where(_dist(tq, tk, t0, k_own) >= 0, 0.0, NEG)
    streams = []
    for hd in range(2):
        sl = slice(hd * LANES, (hd + 1) * LANES)
        q = q_ref[0, :, sl]
        s_own = _dot_nt(q, k_ref[0, pl.ds(k_own, tk), sl]) * MLA_SCALE + bias
        streams.append(dict(q=q, k_rows=_rows_at(k_ref, sl), v_rows=_rows_at(v_ref, sl), s_own=s_own,
                            v_own=v_ref[0, pl.ds(k_own, tk), sl]))
    accs = _two_pass(streams, n_past, tk, s_sc, scale=MLA_SCALE)
    o_ref[0] = jnp.concatenate([_finish_ones(acc) for acc in accs], axis=1).astype(o_ref.dtype)


def _mla(qB, kB, vB, tq=512, tk=512):
    b, s, _ = qB.shape
    tk = min(tk, s)
    nq = s // tq
    resident = lambda: pl.BlockSpec((1, s, 2 * LANES), lambda bi, p, i: (bi, 0, p), pipeline_mode=pl.Buffered(1))
    return pl.pallas_call(
        functools.partial(_mla_kernel, tq=tq, tk=tk),
        grid=(b, 2, nq),
        in_specs=[pl.BlockSpec((1, tq, 2 * LANES), lambda bi, p, i: (bi, i, p)), resident(), resident()],
        out_specs=pl.BlockSpec((1, tq, LANES), lambda bi, p, i: (bi, i, p)),
        out_shape=jax.ShapeDtypeStruct((b, s, N_HEADS * HEAD_DIM), BF16),
        scratch_shapes=[pltpu.VMEM((2, max(s // tk - 1, 1), tq, tk), F32)],
        compiler_params=_cparams(("parallel", "parallel", "arbitrary")),
        name="mla_attn",
    )(qB, kB, vB)


def _gelu_tanh(x):
    return x * (0.5 * (1.0 + jnp.tanh(math.sqrt(2.0 / math.pi) * (x + 0.044715 * (x * x * x)))))


def _nsa_cmp_kernel(ch_ref, w1_ref, pos_ref, w1f_ref, w2_ref, kc_ref, vc_ref):
    nc = ch_ref.shape[1]
    u = _dot(ch_ref[0], w1_ref[...])
    for idx, out in ((0, kc_ref), (1, vc_ref)):
        pos = jnp.broadcast_to(pos_ref[idx], (8, pos_ref.shape[2]))
        w1f = w1f_ref[idx]
        p_hi = pos.astype(BF16)
        p_lo = (pos - p_hi.astype(F32)).astype(BF16)
        w_hi = w1f.astype(BF16)
        w_lo = (w1f - w_hi.astype(F32)).astype(BF16)
        posb = (_dot(p_hi, w_hi) + _dot(p_hi, w_lo) + _dot(p_lo, w_hi))[0:1, :]
        top = u[:, (2 * idx) * LANES:(2 * idx + 1) * LANES]
        bot = u[:, (2 * idx + 1) * LANES:(2 * idx + 2) * LANES]
        hid = top + pltpu.roll(bot, nc - 1, 0) + posb
        res = _dot(_gelu_tanh(hid).astype(BF16), w2_ref[idx])
        if idx == 0:
            lane = lax.broadcasted_iota(jnp.int32, (nc, LANES), 1)
            cend = lax.broadcasted_iota(jnp.int32, (nc, LANES), 0) * NSA_CMP_STRIDE + (NSA_CMP_LEN - 1)
            res = res + _pos_lanes(lane, cend)
        out[0] = res.astype(out.dtype)


def _nsa_compress(ch, w1aug, pos2, w1f, w2p):
    b, nc, w = ch.shape
    full = lambda a: pl.BlockSpec(a.shape, lambda i: (0,) * a.ndim)
    return pl.pallas_call(
        _nsa_cmp_kernel,
        grid=(b,),
        in_specs=[pl.BlockSpec((1, nc, w), lambda i: (i, 0, 0)), full(w1aug), full(pos2), full(w1f), full(w2p)],
        out_specs=[pl.BlockSpec((1, nc, LANES), lambda i: (i, 0, 0))] * 2,
        out_shape=[jax.ShapeDtypeStruct((b, nc, LANES), BF16)] * 2,
        compiler_params=_cparams(("parallel",)),
        name="nsa_compress",
    )(ch, w1aug, pos2, w1f, w2p)


NSA_CHUNK = 256


def _nsa_kernel(q_ref, kc_ref, vc_ref, kv_ref, gl_ref, mm_ref, o_ref, imp_sc, s_sc, act_sm):
    n = pl.program_id(1)
    t0 = n * QB
    nc = kc_ref.shape[1]
    nbl = LANES
    qh = [q_ref[0, :, hd * LANES:(hd + 1) * LANES] for hd in range(N_HEADS)]

    def stack(x):
        return jnp.concatenate([x] * N_HEADS, axis=0)

    qs = jnp.concatenate(qh, axis=0)

    slab = NSA_WINDOW + QB
    k0w = pl.multiple_of(jnp.maximum(n - NSA_WINDOW // QB, 0) * QB, QB)
    dist = _dist(QB, slab, t0, k0w)
    rblk = 64
    n_rblk = N_HEADS * QB // rblk
    bias_w = jnp.where(jnp.logical_and(dist >= 0, dist < NSA_WINDOW), 0.0, NEG)
    s = _dot_nt(qs, kv_ref[0, pl.ds(k0w, slab), 3 * LANES:4 * LANES])
    blocks = []
    for rb in range(n_rblk):
        qo = (rb * rblk) % QB
        blk = s[rb * rblk:(rb + 1) * rblk] + bias_w[qo:qo + rblk]
        blocks.append(jnp.exp(blk - jnp.max(blk, axis=1, keepdims=True)).astype(BF16))
    o_win = _finish_ones(_dot(jnp.concatenate(blocks, axis=0), kv_ref[0, pl.ds(k0w, slab), 4 * LANES:5 * LANES]))

    tq1 = lax.broadcasted_iota(jnp.int32, (QB, 1), 0) + t0
    cidx = lax.broadcasted_iota(jnp.int32, (QB, nc), 1)
    ok = jnp.logical_and(tq1 >= cidx * NSA_CMP_STRIDE + (NSA_CMP_LEN - 1), cidx < nc - 1)
    bias_c = jnp.where(ok, 0.0, NEG)
    row_ok = jnp.where(tq1 >= NSA_CMP_LEN - 1, 1.0, 0.0)
    s = _dot_nt(qs, kc_ref[0])
    blocks = []
    psum_parts = [None] * (QB // rblk)
    for rb in range(n_rblk):
        qo = (rb * rblk) % QB
        blk = s[rb * rblk:(rb + 1) * rblk] + bias_c[qo:qo + rblk]
        e = jnp.exp(blk - jnp.max(blk, axis=1, keepdims=True))
        p = e * (row_ok[qo:qo + rblk] / jnp.maximum(jnp.sum(e, axis=1, keepdims=True), TINY))
        blocks.append(p.astype(BF16))
        part = psum_parts[qo // rblk]
        psum_parts[qo // rblk] = p if part is None else part + p
    o_cmp = _dot(jnp.concatenate(blocks, axis=0), vc_ref[0])
    psum = jnp.concatenate(psum_parts, axis=0)

    mm = mm_ref[...]
    imp = None
    for piece in _split3(psum):
        d = _dot(piece, mm)
        imp = d if imp is None else imp + d
    imp_t = imp.T
    jrow = lax.broadcasted_iota(jnp.int32, (nbl, QB), 0)
    tcol = lax.broadcasted_iota(jnp.int32, (nbl, QB), 1) + t0
    cur = tcol >> 6
    cand = jrow <= cur
    forced = jnp.logical_or(jnp.logical_or(jrow == 0, jrow == cur), jrow == cur - 1)
    a = jnp.where(cand, jnp.where(forced, BIG, imp_t), NEG)
    key = lax.bitcast_convert_type(a, jnp.int32)
    key1 = key + 1
    imp_sc[...] = key
    n_cand = (t0 + QB - 1) // NSA_SEL_BLOCK + 1

    def rank_body(i2, cnt):
        for u in range(2):
            i = 2 * i2 + u
            row = imp_sc[pl.ds(i, 1), :]
            cnt = cnt + jnp.where(row >= jnp.where(jrow <= i, key1, key), 1.0, 0.0)
        return cnt

    rank = lax.fori_loop(0, n_cand // 2, rank_body, jnp.zeros((nbl, QB), F32))
    sel_t = jnp.where(jnp.logical_and(rank < NSA_TOPN, cand), 0.0, NEG)
    selneg = sel_t.T.astype(BF16)

    q_sel = jnp.concatenate([jnp.concatenate([qh[hd], selneg], axis=1) for hd in range(N_HEADS)], axis=0)
    own = t0 // NSA_CHUNK
    k_own = pl.multiple_of(own * NSA_CHUNK, NSA_CHUNK)
    ksl, vsl = slice(0, 2 * LANES), slice(2 * LANES, 3 * LANES)
    s_own = (_dot_nt(q_sel, kv_ref[0, pl.ds(k_own, NSA_CHUNK), ksl])
             + stack(jnp.where(_dist(QB, NSA_CHUNK, t0, k_own) >= 0, 0.0, NEG)))
    n_chunks = kv_ref.shape[1] // NSA_CHUNK
    blk_shift = (NSA_CHUNK // NSA_SEL_BLOCK).bit_length() - 1
    grp = jnp.where((lax.broadcasted_iota(jnp.int32, (n_chunks, nbl), 1) >> blk_shift)
                    == lax.broadcasted_iota(jnp.int32, (n_chunks, nbl), 0), 1.0, 0.0).astype(BF16)
    picked = jnp.where(sel_t == 0.0, 1.0, 0.0).astype(BF16)
    cnt = _dot(_dot(grp, picked).astype(BF16), jnp.ones((QB, LANES), BF16))
    n_act = jnp.int32(0)
    for c in range(n_chunks):
        act_sm[n_act] = c
        n_act = n_act + jnp.where(jnp.logical_and(cnt[c, 0] > 0.0, c < own), 1, 0)
    stream = dict(q=q_sel, k_rows=_rows_at(kv_ref, ksl), v_rows=_rows_at(kv_ref, vsl),
                  s_own=s_own, v_own=kv_ref[0, pl.ds(k_own, NSA_CHUNK), vsl])
    o_sel = _finish_ones(_two_pass([stream], n_act, NSA_CHUNK, s_sc, groups=(4, 2, 1),
                                   chunk_of=lambda pos: act_sm[pos])[0])

    g = jax.nn.sigmoid(gl_ref[0])
    outs = []
    for hd in range(N_HEADS):
        rs = slice(hd * QB, (hd + 1) * QB)
        outs.append(g[:, hd:hd + 1] * o_cmp[rs, :HEAD_DIM] + g[:, N_HEADS + hd:N_HEADS + hd + 1] * o_sel[rs]
                    + g[:, 2 * N_HEADS + hd:2 * N_HEADS + hd + 1] * o_win[rs])
    o_ref[0] = jnp.concatenate(outs, axis=1).astype(o_ref.dtype)


def _nsa(qC, kcmp, vcmp, kvC, gl, m_mat):
    b, s, _ = qC.shape
    nq = s // QB
    nc = kcmp.shape[1]
    assert s >= NSA_WINDOW + QB and s % NSA_CHUNK == 0
    return pl.pallas_call(
        _nsa_kernel,
        grid=(b, nq),
        in_specs=[pl.BlockSpec((1, QB, N_HEADS * LANES), lambda i, j: (i, j, 0)),
                  pl.BlockSpec((1, nc, LANES), lambda i, j: (i, 0, 0)),
                  pl.BlockSpec((1, nc, LANES), lambda i, j: (i, 0, 0)),
                  pl.BlockSpec((1, s, 5 * LANES), lambda i, j: (i, 0, 0), pipeline_mode=pl.Buffered(1)),
                  pl.BlockSpec((1, QB, LANES), lambda i, j: (i, j, 0)),
                  pl.BlockSpec(m_mat.shape, lambda i, j: (0, 0))],
        out_specs=pl.BlockSpec((1, QB, N_HEADS * HEAD_DIM), lambda i, j: (i, j, 0)),
        out_shape=jax.ShapeDtypeStruct((b, s, N_HEADS * HEAD_DIM), BF16),
        scratch_shapes=[pltpu.VMEM((LANES, QB), jnp.int32),
                        pltpu.VMEM((1, max(s // NSA_CHUNK - 1, 1), N_HEADS * QB, NSA_CHUNK), F32),
                        pltpu.SMEM((s // NSA_CHUNK + 1,), jnp.int32)],
        compiler_params=_cparams(("parallel", "arbitrary")),
        name="nsa_attn",
    )(qC, kcmp, vcmp, kvC, gl, m_mat)


MOBA_NBR = LANES - MOBA_LANE0


MOBA_TQ = 2 * MOBA_BLOCK


def _moba_kernel(q_ref, k_ref, v_ref, o_ref, kmean_sc, gate_sc, s_sc):
    tq = MOBA_TQ
    n = pl.program_id(2)
    t0 = n * tq
    n_blk = k_ref.shape[1] // MOBA_BLOCK
    lane = lax.broadcasted_iota(jnp.int32, (1, LANES), 1)

    @pl.when(n == 0)
    def _():
        kmean_sc[...] = jnp.zeros(kmean_sc.shape, F32)
        for hd in range(2):
            for jb in range(n_blk):
                blk = k_ref[0, jb * MOBA_BLOCK:(jb + 1) * MOBA_BLOCK, hd * LANES:(hd + 1) * LANES].astype(F32)
                kmean_sc[hd, jb:jb + 1, :] = jnp.where(lane < HEAD_DIM, jnp.mean(blk, axis=0, keepdims=True), 0.0)

    jrow = lax.broadcasted_iota(jnp.int32, (MOBA_NBR, tq), 0)
    qcol = lax.broadcasted_iota(jnp.int32, (MOBA_NBR, tq), 1)
    cur = 2 * n + (qcol >> MOBA_SHIFT)
    qs = [q_ref[0, :, hd * LANES:(hd + 1) * LANES] for hd in range(2)]
    gates = []
    for hd in range(2):
        gate_t = _dot_nt(kmean_sc[hd].astype(BF16), qs[hd])[0:MOBA_NBR, :]
        a = jnp.where(jrow < cur, gate_t, NEG)
        gate_sc[hd] = a
        gates.append(a)

    def rank_body(i, cnts):
        out = []
        for hd in range(2):
            row = gate_sc[hd, pl.ds(i, 1), :]
            c_ge = jnp.where(row >= gates[hd], 1.0, 0.0)
            c_gt = jnp.where(row > gates[hd], 1.0, 0.0)
            out.append(cnts[hd] + jnp.where(jrow > i, c_ge, c_gt))
        return tuple(out)

    ranks = lax.fori_loop(0, 2 * n + 1, rank_body, tuple(jnp.zeros((MOBA_NBR, tq), F32) for _ in range(2)))
    tk = MOBA_TQ
    k_own = pl.multiple_of(t0, tk)
    bias = jnp.where(_dist(tq, tk, 0, 0) >= 0, 0.0, NEG)
    streams = []
    for hd in range(2):
        sl = slice(hd * LANES, (hd + 1) * LANES)
        keep = jnp.logical_or(jnp.logical_and(ranks[hd] < MOBA_TOPK, jrow < cur), jrow == cur)
        full_t = jnp.concatenate([jnp.zeros((MOBA_LANE0, tq), F32), jnp.where(keep, 0.0, NEG)], axis=0)
        q_sel = (qs[hd].astype(F32) + full_t.T).astype(BF16)
        s_own = _dot_nt(q_sel, k_ref[0, pl.ds(k_own, tk), sl]) + bias
        streams.append(dict(q=q_sel, k_rows=_rows_at(k_ref, sl), v_rows=_rows_at(v_ref, sl), s_own=s_own,
                            v_own=v_ref[0, pl.ds(k_own, tk), sl]))
    accs = _two_pass(streams, n, tk, s_sc)
    o_ref[0] = jnp.concatenate([_finish_ones(acc) for acc in accs], axis=1).astype(o_ref.dtype)


def _moba(qD, kD, vD):
    b, s, _ = qD.shape
    tq = MOBA_TQ
    assert s % tq == 0 and s // MOBA_BLOCK <= MOBA_NBR
    resident = lambda: pl.BlockSpec((1, s, 2 * LANES), lambda i, p, j: (i, 0, p), pipeline_mode=pl.Buffered(1))
    return pl.pallas_call(
        _moba_kernel,
        grid=(b, 2, s // tq),
        in_specs=[pl.BlockSpec((1, tq, 2 * LANES), lambda i, p, j: (i, j, p)), resident(), resident()],
        out_specs=pl.BlockSpec((1, tq, LANES), lambda i, p, j: (i, j, p)),
        out_shape=jax.ShapeDtypeStruct((b, s, N_HEADS * HEAD_DIM), BF16),
        scratch_shapes=[pltpu.VMEM((2, LANES, LANES), F32), pltpu.VMEM((2, MOBA_NBR, tq), F32),
                        pltpu.VMEM((2, max(s // tq - 1, 1), tq, tq), F32)],
        compiler_params=_cparams(("parallel", "parallel", "arbitrary")),
        name="moba_attn",
    )(qD, kD, vD)


def _post_kernel(oa_ref, ob_ref, oc_ref, od_ref, x_ref, mod_ref, wo_ref, g_ref, wu_ref, wd_ref, fg_ref, out_ref,
                 *, final, ffc):
    x = x_ref[0]
    mix = None
    for idx, ref in enumerate((oa_ref, ob_ref, oc_ref, od_ref)):
        d = _dot(ref[0], wo_ref[idx])
        mix = d if mix is None else mix + d
    x1 = x + mod_ref[0, 2:3, :] * mix
    y = x1 * lax.rsqrt(jnp.mean(x1 * x1, axis=-1, keepdims=True) + RMS_EPS) * g_ref[...]
    hb = (y * (1.0 + mod_ref[0, 4:5, :]) + mod_ref[0, 3:4, :]).astype(BF16)
    acc = None
    for c in range(wu_ref.shape[1] // ffc):
        hid = jnp.maximum(_dot(hb, wu_ref[:, c * ffc:(c + 1) * ffc]), 0.0)
        d = _dot((hid * hid).astype(BF16), wd_ref[c * ffc:(c + 1) * ffc, :])
        acc = d if acc is None else acc + d
    x2 = x1 + mod_ref[0, 5:6, :] * acc
    if final:
        x2 = x2 * lax.rsqrt(jnp.mean(x2 * x2, axis=-1, keepdims=True) + RMS_EPS) * fg_ref[...]
    out_ref[0] = x2


def _post(oa, ob, oc, od, x, mod, wo4, g, wu, wd, fg, final, ts=512, ffc=1024):
    b, s, d = x.shape
    nt = s // ts
    tok = lambda w: pl.BlockSpec((1, ts, w), lambda i, j: (i, j, 0))
    full = lambda a: pl.BlockSpec(a.shape, lambda i, j: (0,) * a.ndim, pipeline_mode=pl.Buffered(1))
    hw = N_HEADS * HEAD_DIM
    return pl.pallas_call(
        functools.partial(_post_kernel, final=final, ffc=ffc),
        grid=(b, nt),
        in_specs=[tok(hw), tok(hw), tok(hw), tok(hw), tok(d), pl.BlockSpec((1, N_ADA, d), lambda i, j: (i, 0, 0)),
                  full(wo4), full(g), full(wu), full(wd), full(fg)],
        out_specs=tok(d),
        out_shape=jax.ShapeDtypeStruct((b, s, d), F32),
        compiler_params=_cparams(("parallel", "parallel")),
        name="post_mlp",
    )(oa, ob, oc, od, x, mod, wo4, g, wu, wd, fg)


def _rot_half_cols(w):
    half = w.shape[1] // 2
    return jnp.concatenate([-w[:, half:], w[:, :half]], axis=1)


def _layer_weights(w_in, mla_qg, mla_kvg, w_uq, w_ukv, pos_k, pos_v, ck_w1, ck_w2, cv_w1, cv_w2):
    d = w_in.shape[0]
    z = lambda n: jnp.zeros((d, n), F32)
    o = 0

    def take(n):
        nonlocal o
        r = w_in[:, o:o + n]
        o += n
        return r

    qa, ka, va = take(256), take(128), take(128)
    cq, ckv, kpe = take(MLA_Q_RANK), take(MLA_KV_RANK), take(MLA_ROPE)
    qc, kc, vc, ks, vs, kw, vw, gl = take(256), take(64), take(64), take(64), take(64), take(64), take(64), take(12)
    qd, kd, vd = take(256), take(256), take(256)
    sc = 1.0 / 8.0
    kpe_main = jnp.concatenate([z(64), kpe, z(32)], axis=1)
    kpe_rot = jnp.concatenate([z(64), _rot_half_cols(kpe), z(32)], axis=1)
    p64 = lambda w: jnp.pad(w, ((0, 0), (0, LANES - w.shape[1])))
    groups = {
        "mla": jnp.concatenate([cq, z(64), ckv, kpe_main, kpe_rot], axis=1),
        "A": jnp.concatenate([qa * sc, ka, va], axis=1),
        "C": jnp.concatenate([qc * sc, kc, vc, ks, vs, kw, vw, p64(gl)], axis=1),
        "D": jnp.concatenate([qd * sc, kd, vd], axis=1),
    }
    wmain = jnp.concatenate([groups[n] for n, _ in _IN_GROUPS], axis=1).astype(BF16)

    dq = MLA_NOPE + MLA_ROPE
    wq_cols, wkv_cols = [], []
    for hd in range(N_HEADS):
        wh = w_uq[:, hd * dq:(hd + 1) * dq]
        zq = lambda n: jnp.zeros((MLA_Q_RANK, n), F32)
        wq_cols += [wh, zq(32), zq(64), _rot_half_cols(wh[:, MLA_NOPE:]), zq(32)]
        wk = w_ukv[:, hd * 128:(hd + 1) * 128]
        zk = jnp.zeros((MLA_KV_RANK, 64), F32)
        wkv_cols += [wk[:, :64], zk, wk[:, 64:], zk]
    wq = jnp.pad(jnp.concatenate(wq_cols, axis=1), ((0, 256 - MLA_Q_RANK), (0, 0))).astype(BF16)
    wkv = jnp.concatenate(wkv_cols, axis=1).astype(BF16)
    qg = jnp.pad(mla_qg, (0, 256 - MLA_Q_RANK)).reshape(1, 256)
    kvg = mla_kvg.reshape(1, MLA_KV_RANK)


    def aug(w1, is_v):
        w = w1.reshape(NSA_CMP_LEN, HEAD_DIM, NSA_CMP_HIDDEN)
        zz = jnp.zeros_like(w)
        w = jnp.concatenate([zz, w] if is_v else [w, zz], axis=1)
        w = w.reshape(NSA_CMP_LEN * LANES, NSA_CMP_HIDDEN)
        return w[:NSA_CMP_STRIDE * LANES], w[NSA_CMP_STRIDE * LANES:]

    kt, kb = aug(ck_w1, False)
    vt, vb = aug(cv_w1, True)
    w1aug = jnp.concatenate([kt, kb, vt, vb], axis=1).astype(BF16)
    pos2 = jnp.stack([pos_k.reshape(1, -1), pos_v.reshape(1, -1)])
    w1f = jnp.stack([ck_w1, cv_w1])
    w2p = jnp.stack([p64(ck_w2), p64(cv_w2)]).astype(BF16)
    return wmain, qg, kvg, wq, wkv, w1aug, pos2, w1f, w2p


def _rope_tables(s):
    half = MLA_ROPE // 2
    freqs = ROPE_THETA ** (-jnp.arange(half, dtype=F32) / half)
    ang = jnp.arange(s, dtype=F32)[:, None] * freqs[None, :]
    cos, sin = jnp.cos(ang), jnp.sin(ang)
    ones, zeros = jnp.ones((s, MLA_NOPE), F32), jnp.zeros((s, MLA_NOPE), F32)
    tail = jnp.zeros((s, LANES - MLA_NOPE - MLA_ROPE), F32)
    return (jnp.concatenate([ones, cos, cos, tail], axis=1), jnp.concatenate([zeros, sin, sin, tail], axis=1))


def _nsa_imp_matrix(s):
    n_sel = s // NSA_SEL_BLOCK
    nc = s // NSA_CMP_STRIDE
    assert n_sel <= LANES
    c = np.arange(nc)[:, None]
    j = np.arange(LANES)[None, :]
    m = ((c >= 4 * j - 1) & (c <= 4 * j + 3) & (c < nc - 1) & (j < n_sel)).astype(np.float32)
    return jnp.asarray(m, BF16)


def kernel(x, c, norm_mix_g, norm_mlp_g, w_ada, b_ada, w_in, w_out, swa_sinks, mla_q_norm_g, mla_kv_norm_g,
           mla_w_uq, mla_w_ukv, nsa_cmp_pos_k, nsa_cmp_pos_v, nsa_cmp_k_w1, nsa_cmp_k_w2, nsa_cmp_v_w1,
           nsa_cmp_v_w2, w_up, w_down, final_norm_g):
    b, s, d = x.shape
    depth = w_in.shape[0]
    assert b <= SUBLANES
    c8 = jnp.pad(c, ((0, SUBLANES - b), (0, 0)))
    mod_all = _ada(c8, w_ada, b_ada)
    cos_t, sin_t = _rope_tables(s)
    m_mat = _nsa_imp_matrix(s)
    fg = final_norm_g.reshape(1, d)
    for l in range(depth):
        mod = mod_all[l, :b].reshape(b, N_ADA, d)
        wmain, qg, kvg, wq, wkv, w1aug, pos2, w1f, w2p = _layer_weights(
            w_in[l], mla_q_norm_g[l], mla_kv_norm_g[l], mla_w_uq[l], mla_w_ukv[l], nsa_cmp_pos_k[l],
            nsa_cmp_pos_v[l], nsa_cmp_k_w1[l], nsa_cmp_k_w2[l], nsa_cmp_v_w1[l], nsa_cmp_v_w2[l])
        z = _in_proj(x, mod, norm_mix_g[l].reshape(1, d), wmain, qg, kvg, wq, wkv, cos_t, sin_t)
        o_a = _swa(z["qA"], z["kA"], z["vA"], swa_sinks[l])
        o_b = _mla(z["qB"], z["kB"], z["vB"])
        ch = z["kcvc"].reshape(b, s // NSA_CMP_STRIDE, NSA_CMP_STRIDE * LANES)
        kcmp, vcmp = _nsa_compress(ch, w1aug, pos2, w1f, w2p)
        o_c = _nsa(z["qC"], kcmp, vcmp, z["kvC"], z["gl"], m_mat)
        o_d = _moba(z["qD"], z["kD"], z["vD"])
        wo4 = w_out[l].astype(BF16).reshape(4, N_HEADS * HEAD_DIM, d)
        x = _post(o_a, o_b, o_c, o_d, x, mod, wo4, norm_mlp_g[l].reshape(1, d), w_up[l].astype(BF16),
                  w_down[l].astype(BF16), fg, final=(l == depth - 1))
    return x
```
